```python
import math
import jax, jax.numpy as jnp
from jax import lax
import numpy as np

D_MODEL = 1024
BATCH = 2
SEQ = 16384
DEPTH = 2
DEC_BATCH = 8
DEC_SEQ = 2048
PAST_LEN = 128

GRID_W = 64
HEAD_DIM = 64
N_EVEN = (DEPTH + 1) // 2
N_ODD = DEPTH // 2
DIFF_HEADS = 4
DIFF_VDIM = 2 * HEAD_DIM
GQA_HEADS = 8
GQA_KV_HEADS = 2
GQA_GROUP = GQA_HEADS // GQA_KV_HEADS
NA_HEADS = D_MODEL // HEAD_DIM
NA_KH_MAX = 8
NA_KW = 16
NA_BLOCK = GRID_W
DIFF_QK_W = DIFF_HEADS * 2 * HEAD_DIM
DIFF_V_W = DIFF_HEADS * DIFF_VDIM
GQA_Q_W = GQA_HEADS * HEAD_DIM
GQA_KV_W = GQA_KV_HEADS * HEAD_DIM
EVEN_SPLITS = [DIFF_QK_W, 2 * DIFF_QK_W, 2 * DIFF_QK_W + DIFF_V_W,
               2 * DIFF_QK_W + DIFF_V_W + GQA_Q_W,
               2 * DIFF_QK_W + DIFF_V_W + GQA_Q_W + GQA_KV_W]
EVEN_IN_W = 2 * DIFF_QK_W + DIFF_V_W + GQA_Q_W + 2 * GQA_KV_W
MIX_W = DIFF_V_W + GQA_Q_W
NA_W = NA_HEADS * HEAD_DIM
D_FF = 2816
CONV_W = 3
Q_BLOCK = 128
ROPE_THETA = 10000.0
EPS = 1e-6
SUBLN_EPS = 1e-5

kernel_name = "hybrid_diffattn_gqa_natten_encoder"


def rmsnorm(x, g, eps=EPS):
    xf = x.astype(jnp.float32)
    y = xf * lax.rsqrt(jnp.mean(xf * xf, axis=-1, keepdims=True) + eps)
    return (y * g.astype(jnp.float32)).astype(x.dtype)


def rope_1d(seq, dim):
    inv = ROPE_THETA ** (-jnp.arange(0, dim, 2, dtype=jnp.float32) / dim)
    t = jnp.arange(seq, dtype=jnp.float32)
    return t[:, None] * inv[None, :]


def rope_axial(seq, dim):
    half = dim // 2
    inv = ROPE_THETA ** (-jnp.arange(0, half, 2, dtype=jnp.float32) / half)
    t = jnp.arange(seq, dtype=jnp.int32)
    row = (t // GRID_W).astype(jnp.float32)
    col = (t % GRID_W).astype(jnp.float32)
    return jnp.concatenate([row[:, None] * inv[None, :], col[:, None] * inv[None, :]], axis=-1)


def apply_rope(x, ang):
    shape = (ang.shape[0],) + (1,) * (x.ndim - 3) + (ang.shape[1],)
    cos = jnp.cos(ang).reshape(shape).astype(x.dtype)
    sin = jnp.sin(ang).reshape(shape).astype(x.dtype)
    x1, x2 = jnp.split(x, 2, axis=-1)
    return jnp.concatenate([x1 * cos - x2 * sin, x2 * cos + x1 * sin], axis=-1)


def even_mixer(h, w_in, w_out, lam_vec, subln_g, qk_g, layer_idx):
    b, s, _ = h.shape
    nb = s // Q_BLOCK
    proj = h @ w_in
    qa, ka, va, qb, kb, vb = jnp.split(proj, EVEN_SPLITS, axis=-1)
    ang1 = rope_1d(s, HEAD_DIM)
    qa = apply_rope(qa.reshape(b, s, DIFF_HEADS, 2, HEAD_DIM), ang1)
    ka = apply_rope(ka.reshape(b, s, DIFF_HEADS, 2, HEAD_DIM), ang1)
    va = va.reshape(b, s, DIFF_HEADS, DIFF_VDIM)
    lambda_init = 0.8 - 0.6 * math.exp(-0.3 * layer_idx)
    lf = lam_vec.astype(jnp.float32)
    lam = jnp.exp(jnp.sum(lf[0] * lf[1])) - jnp.exp(jnp.sum(lf[2] * lf[3])) + lambda_init
    ang2 = rope_axial(s, HEAD_DIM)
    qb = apply_rope(rmsnorm(qb.reshape(b, s, GQA_KV_HEADS, GQA_GROUP, HEAD_DIM), qk_g[0]), ang2)
    kb = apply_rope(rmsnorm(kb.reshape(b, s, GQA_KV_HEADS, HEAD_DIM), qk_g[1]), ang2)
    vb = vb.reshape(b, s, GQA_KV_HEADS, HEAD_DIM)
    scale = HEAD_DIM ** -0.5

    def block(qs):
        qa_b, qb_b = qs
        sa = jnp.einsum('bqhcd,bkhcd->bhcqk', qa_b, ka).astype(jnp.float32) * scale
        pa = jax.nn.softmax(sa, axis=-1)
        pdiff = pa[:, :, 0] - lam * pa[:, :, 1]
        oa = jnp.einsum('bhqk,bkhe->bqhe', pdiff.astype(va.dtype), va)
        oa = rmsnorm(oa, subln_g, eps=SUBLN_EPS) * (1.0 - lambda_init)
        sb = jnp.einsum('bqngd,bknd->bngqk', qb_b, kb).astype(jnp.float32) * scale
        pb = jax.nn.softmax(sb, axis=-1)
        ob = jnp.einsum('bngqk,bknd->bqngd', pb.astype(vb.dtype), vb)
        return jnp.concatenate([oa.reshape(b, Q_BLOCK, DIFF_V_W),
                                ob.reshape(b, Q_BLOCK, GQA_Q_W)], axis=-1)

    qa_blk = qa.reshape(b, nb, Q_BLOCK, DIFF_HEADS, 2, HEAD_DIM).swapaxes(0, 1)
    qb_blk = qb.reshape(b, nb, Q_BLOCK, GQA_KV_HEADS, GQA_GROUP, HEAD_DIM).swapaxes(0, 1)
    o = lax.map(block, (qa_blk, qb_blk))
    o = o.swapaxes(0, 1).reshape(b, s, MIX_W)
    return o @ w_out


def na_indices(s):
    rows = s // GRID_W
    kh = min(NA_KH_MAX, rows)
    t = jnp.arange(s, dtype=jnp.int32)
    r = t // GRID_W
    col = t % GRID_W
    rs = jnp.clip(r - kh // 2, 0, rows - kh)
    cs = jnp.clip(col - NA_KW // 2, 0, GRID_W - NA_KW)
    kr = rs[:, None, None] + jnp.arange(kh, dtype=jnp.int32)[None, :, None]
    kc = cs[:, None, None] + jnp.arange(NA_KW, dtype=jnp.int32)[None, None, :]
    idx = (kr * GRID_W + kc).reshape(s, kh * NA_KW)
    dr = kr - r[:, None, None] + (NA_KH_MAX - 1)
    dc = kc - col[:, None, None] + (NA_KW - 1)
    bias_idx = (dr * (2 * NA_KW - 1) + dc).reshape(s, kh * NA_KW)
    return idx, bias_idx


def odd_mixer(h, w_qkv, rpb, w_out):
    b, s, _ = h.shape
    nb = s // NA_BLOCK
    qkv = (h @ w_qkv).reshape(b, s, 3, NA_HEADS, HEAD_DIM)
    q, k, v = qkv[:, :, 0], qkv[:, :, 1], qkv[:, :, 2]
    idx, bias_idx = na_indices(s)
    n_keys = idx.shape[-1]
    rpb_flat = rpb.reshape(NA_HEADS, -1)
    scale = HEAD_DIM ** -0.5

    def block(xs):
        q_b, idx_b, bidx_b = xs
        kg = jnp.take(k, idx_b, axis=1)
        vg = jnp.take(v, idx_b, axis=1)
        bias = rpb_flat[:, bidx_b].astype(jnp.float32)
        sc = jnp.einsum('bqhd,bqkhd->bhqk', q_b, kg).astype(jnp.float32) * scale + bias[None]
        p = jax.nn.softmax(sc, axis=-1)
        o = jnp.einsum('bhqk,bqkhd->bqhd', p.astype(v.dtype), vg)
        return o.reshape(b, NA_BLOCK, NA_W)

    q_blk = q.reshape(b, nb, NA_BLOCK, NA_HEADS, HEAD_DIM).swapaxes(0, 1)
    o = lax.map(block, (q_blk, idx.reshape(nb, NA_BLOCK, n_keys),
                        bias_idx.reshape(nb, NA_BLOCK, n_keys)))
    o = o.swapaxes(0, 1).reshape(b, s, NA_W)
    return o @ w_out


def conv_ffn(h, w_up, conv_w, conv_b, w_down):
    u = h @ w_up
    up = jnp.pad(u, ((0, 0), (1, 1), (0, 0)))
    u = up[:, :-2] * conv_w[0] + up[:, 1:-1] * conv_w[1] + up[:, 2:] * conv_w[2] + conv_b
    gate, val = jnp.split(u, 2, axis=-1)
    return (jax.nn.silu(gate) * val) @ w_down


def trunk(x, c, ada_w, ada_b, norm_g, even_w_in, even_w_out, diff_lambda, diff_subln_g,
          gqa_qk_g, odd_w_qkv, odd_rpb, odd_w_out, ffn_w_up, ffn_conv_w, ffn_conv_b,
          ffn_w_down, final_g):
    c_act = jax.nn.silu(c)
    for l in range(DEPTH):
        mod = c_act @ ada_w[l] + ada_b[l]
        sh1, sc1, g1, sh2, sc2, g2 = [m[:, None, :] for m in jnp.split(mod, 6, axis=-1)]
        h = rmsnorm(x, norm_g[l, 0]) * (1.0 + sc1) + sh1
        if l % 2 == 0:
            e = l // 2
            m = even_mixer(h, even_w_in[e], even_w_out[e], diff_lambda[e], diff_subln_g[e],
                           gqa_qk_g[e], l)
        else:
            o = l // 2
            m = odd_mixer(h, odd_w_qkv[o], odd_rpb[o], odd_w_out[o])
        x = x + g1 * m
        h = rmsnorm(x, norm_g[l, 1]) * (1.0 + sc2) + sh2
        x = x + g2 * conv_ffn(h, ffn_w_up[l], ffn_conv_w[l], ffn_conv_b[l], ffn_w_down[l])
    return rmsnorm(x, final_g)


def setup_inputs(seed: int = 0) -> dict:
    key = jax.random.key(seed)
    ks = jax.random.split(key, 20)
    D = D_MODEL

    def nrm(k, shape, scale):
        return jax.random.normal(k, shape, jnp.float32) * scale

    return {
        "x_prompt": nrm(ks[0], (BATCH, SEQ, D), 1.0),
        "x_sample": nrm(ks[1], (DEC_BATCH, DEC_SEQ, D), 1.0),
        "c_prompt": nrm(ks[2], (BATCH, D), 1.0),
        "c_sample": nrm(ks[3], (DEC_BATCH, D), 1.0),
        "ada_w": nrm(ks[4], (DEPTH, D, 6 * D), D ** -0.5),
        "ada_b": nrm(ks[5], (DEPTH, 6 * D), 0.02),
        "norm_g": 1.0 + nrm(ks[6], (DEPTH, 2, D), 0.02),
        "even_w_in": nrm(ks[7], (N_EVEN, D, EVEN_IN_W), D ** -0.5),
        "even_w_out": nrm(ks[8], (N_EVEN, MIX_W, D), MIX_W ** -0.5),
        "diff_lambda": nrm(ks[9], (N_EVEN, 4, HEAD_DIM), 0.1),
        "diff_subln_g": 1.0 + nrm(ks[10], (N_EVEN, DIFF_VDIM), 0.02),
        "gqa_qk_g": 1.0 + nrm(ks[11], (N_EVEN, 2, HEAD_DIM), 0.02),
        "odd_w_qkv": nrm(ks[12], (N_ODD, D, 3 * NA_W), D ** -0.5),
        "odd_rpb": nrm(ks[13], (N_ODD, NA_HEADS, 2 * NA_KH_MAX - 1, 2 * NA_KW - 1), 0.5),
        "odd_w_out": nrm(ks[14], (N_ODD, NA_W, D), NA_W ** -0.5),
        "ffn_w_up": nrm(ks[15], (DEPTH, D, 2 * D_FF), D ** -0.5),
        "ffn_conv_w": nrm(ks[16], (DEPTH, CONV_W, 2 * D_FF), CONV_W ** -0.5),
        "ffn_conv_b": nrm(ks[17], (DEPTH, 2 * D_FF), 0.02),
        "ffn_w_down": nrm(ks[18], (DEPTH, D_FF, D), D_FF ** -0.5),
        "final_g": 1.0 + nrm(ks[19], (D,), 0.02),
    }


def reference(x_prompt, x_sample, c_prompt, c_sample, ada_w, ada_b, norm_g, even_w_in,
              even_w_out, diff_lambda, diff_subln_g, gqa_qk_g, odd_w_qkv, odd_rpb, odd_w_out,
              ffn_w_up, ffn_conv_w, ffn_conv_b, ffn_w_down, final_g):
    y_prompt = trunk(x_prompt, c_prompt, ada_w, ada_b, norm_g, even_w_in, even_w_out,
                     diff_lambda, diff_subln_g, gqa_qk_g, odd_w_qkv, odd_rpb, odd_w_out,
                     ffn_w_up, ffn_conv_w, ffn_conv_b, ffn_w_down, final_g)
    y_sample = trunk(x_sample, c_sample, ada_w, ada_b, norm_g, even_w_in, even_w_out,
                     diff_lambda, diff_subln_g, gqa_qk_g, odd_w_qkv, odd_rpb, odd_w_out,
                     ffn_w_up, ffn_conv_w, ffn_conv_b, ffn_w_down, final_g)
    return (y_prompt, y_sample)
```

```python
import functools
import math

import jax
import jax.numpy as jnp
from jax import lax
from jax.experimental import pallas as pl
from jax.experimental.pallas import tpu as pltpu

D_MODEL = 1024
GRID_W = 64
HEAD_DIM = 64
HALF = HEAD_DIM // 2
DIFF_HEADS = 4
DIFF_VDIM = 2 * HEAD_DIM
GQA_HEADS = 8
GQA_KV_HEADS = 2
GQA_GROUP = GQA_HEADS // GQA_KV_HEADS
NA_HEADS = D_MODEL // HEAD_DIM
NA_KH = 8
NA_KW = 16
DIFF_QK_W = DIFF_HEADS * 2 * HEAD_DIM
DIFF_V_W = DIFF_HEADS * DIFF_VDIM
GQA_Q_W = GQA_HEADS * HEAD_DIM
GQA_KV_W = GQA_KV_HEADS * HEAD_DIM
EVEN_IN_W = 2 * DIFF_QK_W + DIFF_V_W + GQA_Q_W + 2 * GQA_KV_W
EVEN_K_W = DIFF_QK_W + GQA_KV_W
EVEN_V_W = DIFF_V_W + GQA_KV_W
EVEN_Q_W = DIFF_QK_W + GQA_Q_W
MIX_W = DIFF_V_W + GQA_Q_W
D_FF = 2816
ROPE_THETA = 10000.0
EPS = 1e-6
SUBLN_EPS = 1e-5
LOG2E = math.log2(math.e)
Q_SCALE = HEAD_DIM ** -0.5 * LOG2E
NEG_BIG = -1e30
N_UNITS = 2 * DIFF_HEADS + GQA_HEADS

SUBLANES = 8
V7X_VMEM_LIMIT = 56 * 1024 * 1024

F32 = jnp.float32
BF16 = jnp.bfloat16
NT_DIMS = (((1,), (1,)), ((), ()))


def _params(sem, vmem=V7X_VMEM_LIMIT):
    return pltpu.CompilerParams(dimension_semantics=sem, vmem_limit_bytes=vmem)


def _norm_mod(x, g, scale1p, shift):
    ms = jnp.mean(x * x, axis=-1, keepdims=True)
    return (x * lax.rsqrt(ms + EPS) * g) * scale1p + shift


def _ada_kernel(c_ref, w_ref, b_ref, o_ref):
    c = c_ref[...]
    act = (c * (1.0 / (1.0 + jnp.exp(-c)))).astype(BF16)
    o_ref[...] = jnp.dot(act, w_ref[...].astype(BF16), preferred_element_type=F32) + b_ref[...]


def _ada_mod(c, ada_w, ada_b):
    depth, d, n = ada_w.shape
    b = c.shape[0]
    bp = -(-b // SUBLANES) * SUBLANES
    cp = jnp.pad(c, ((0, bp - b), (0, 0)))
    tn = 1536
    out = pl.pallas_call(
        _ada_kernel,
        out_shape=jax.ShapeDtypeStruct((depth, bp, n), F32),
        grid=(depth, n // tn),
        in_specs=[
            pl.BlockSpec((bp, d), lambda l, j: (0, 0)),
            pl.BlockSpec((None, d, tn), lambda l, j: (l, 0, j)),
            pl.BlockSpec((None, 1, tn), lambda l, j: (l, 0, j)),
        ],
        out_specs=pl.BlockSpec((None, bp, tn), lambda l, j: (l, 0, j)),
        compiler_params=_params(("parallel", "parallel")),
        name="ada_mod",
    )(cp, ada_w, ada_b.reshape(depth, 1, n))
    return out[:, :b]


def _rope_fm(blk, cos, sin):
    x1 = blk[0:HALF]
    x2 = blk[HALF:HEAD_DIM]
    return jnp.concatenate([x1 * cos - x2 * sin, x2 * cos + x1 * sin], axis=0)


def _rms_fm(blk, g):
    ms = jnp.mean(blk * blk, axis=0, keepdims=True)
    return blk * lax.rsqrt(ms + EPS) * g


def _even_proj_kernel(x_ref, sc_ref, sh_ref, g_ref, wt_ref, c1_ref, s1_ref, c2_ref, s2_ref,
                      gq_ref, gk_ref, qt_ref, k_ref, vt_ref, pt_ref, kt_ref):
    h = _norm_mod(x_ref[...], g_ref[...], 1.0 + sc_ref[...], sh_ref[...]).astype(BF16)
    pt_ref[...] = lax.dot_general(wt_ref[...], h, NT_DIMS, preferred_element_type=F32)
    c1, s1, c2, s2 = c1_ref[...], s1_ref[...], c2_ref[...], s2_ref[...]
    gq, gk = gq_ref[...], gk_ref[...]
    qa0, ka0, va0 = 0, DIFF_QK_W, 2 * DIFF_QK_W
    qb0 = va0 + DIFF_V_W
    kb0 = qb0 + GQA_Q_W
    vb0 = kb0 + GQA_KV_W
    for j in range(DIFF_QK_W // HEAD_DIM):
        r = j * HEAD_DIM
        q = _rope_fm(pt_ref[qa0 + r:qa0 + r + HEAD_DIM, :], c1, s1) * Q_SCALE
        qt_ref[r:r + HEAD_DIM, :] = q.astype(BF16)
        kt_ref[r:r + HEAD_DIM, :] = _rope_fm(pt_ref[ka0 + r:ka0 + r + HEAD_DIM, :], c1, s1)
    for j in range(GQA_HEADS):
        r = j * HEAD_DIM
        q = _rope_fm(_rms_fm(pt_ref[qb0 + r:qb0 + r + HEAD_DIM, :], gq), c2, s2) * Q_SCALE
        qt_ref[DIFF_QK_W + r:DIFF_QK_W + r + HEAD_DIM, :] = q.astype(BF16)
    for j in range(GQA_KV_HEADS):
        r = j * HEAD_DIM
        kt_ref[DIFF_QK_W + r:DIFF_QK_W + r + HEAD_DIM, :] = _rope_fm(
            _rms_fm(pt_ref[kb0 + r:kb0 + r + HEAD_DIM, :], gk), c2, s2)
    k_ref[...] = kt_ref[...].T.astype(BF16)
    vt_ref[0:DIFF_V_W, :] = pt_ref[va0:va0 + DIFF_V_W, :].astype(BF16)
    vt_ref[DIFF_V_W:EVEN_V_W, :] = pt_ref[vb0:vb0 + GQA_KV_W, :].astype(BF16)


def _even_proj(x, sc, sh, g, w_in_t, rope, gq, gk, tm):
    b, s, d = x.shape
    c1, s1, c2, s2 = rope
    row = lambda bi, i: (bi, 0, 0)
    tab = pl.BlockSpec((HALF, tm), lambda bi, i: (0, i))
    full2 = lambda shp: pl.BlockSpec(shp, lambda bi, i: (0, 0))
    return pl.pallas_call(
        _even_proj_kernel,
        out_shape=(jax.ShapeDtypeStruct((b, EVEN_Q_W, s), BF16),
                   jax.ShapeDtypeStruct((b, s, EVEN_K_W), BF16),
                   jax.ShapeDtypeStruct((b, EVEN_V_W, s), BF16)),
        grid=(b, s // tm),
        in_specs=[
            pl.BlockSpec((None, tm, d), lambda bi, i: (bi, i, 0)),
            pl.BlockSpec((None, 1, d), row),
            pl.BlockSpec((None, 1, d), row),
            full2((1, d)),
            full2((EVEN_IN_W, d)),
            tab, tab, tab, tab,
            full2((HEAD_DIM, 1)),
            full2((HEAD_DIM, 1)),
        ],
        out_specs=(pl.BlockSpec((None, EVEN_Q_W, tm), lambda bi, i: (bi, 0, i)),
                   pl.BlockSpec((None, tm, EVEN_K_W), lambda bi, i: (bi, i, 0)),
                   pl.BlockSpec((None, EVEN_V_W, tm), lambda bi, i: (bi, 0, i))),
        scratch_shapes=[pltpu.VMEM((EVEN_IN_W, tm), F32), pltpu.VMEM((EVEN_K_W, tm), F32)],
        compiler_params=_params(("parallel", "parallel")),
        name="even_proj",
    )(x, sc, sh, g, w_in_t, c1, s1, c2, s2, gq, gk)


def _even_attn_kernel(qt_ref, k_ref, vt_ref, lam_ref, sg_ref, o_ref,
                      qd_ref, qb_ref, m_ref, l_ref, accd_ref, accb_ref, *, tq, lambda_init):
    kv = pl.program_id(2)

    @pl.when(kv == 0)
    def _init():
        top = lax.broadcasted_iota(jnp.int32, (2 * HEAD_DIM, tq), 0) < HEAD_DIM
        zero = jnp.zeros((2 * HEAD_DIM, tq), BF16)
        for h in range(DIFF_HEADS):
            qh = qt_ref[h * 2 * HEAD_DIM:(h + 1) * 2 * HEAD_DIM, :]
            qd_ref[h, :, 0:tq] = jnp.where(top, qh, zero)
            qd_ref[h, :, tq:2 * tq] = jnp.where(top, zero, qh)
        for n in range(GQA_KV_HEADS):
            for g in range(GQA_GROUP):
                r = DIFF_QK_W + (n * GQA_GROUP + g) * HEAD_DIM
                qg = jnp.concatenate([qt_ref[r:r + HEAD_DIM, :]] * 2, axis=0)
                keep = top if n == 0 else jnp.logical_not(top)
                qb_ref[n, :, g * tq:(g + 1) * tq] = jnp.where(keep, qg, zero)
        m_ref[...] = jnp.full(m_ref.shape, NEG_BIG, F32)
        l_ref[...] = jnp.zeros(l_ref.shape, F32)
        accd_ref[...] = jnp.zeros(accd_ref.shape, F32)
        accb_ref[...] = jnp.zeros(accb_ref.shape, F32)

    def unit(u, st, vt, acc_ref, idx):
        m_old = m_ref[u:u + 1, :]
        m_new = jnp.maximum(m_old, jnp.max(st, axis=0, keepdims=True))
        alpha = jnp.exp2(m_old - m_new)
        p = jnp.exp2(st - m_new)
        l_ref[u:u + 1, :] = alpha * l_ref[u:u + 1, :] + jnp.sum(p, axis=0, keepdims=True)
        m_ref[u:u + 1, :] = m_new
        pv = jnp.dot(vt, p.astype(BF16), preferred_element_type=F32)
        acc_ref[idx] = alpha * acc_ref[idx] + pv

    for h in range(DIFF_HEADS):
        kh = k_ref[:, h * 2 * HEAD_DIM:(h + 1) * 2 * HEAD_DIM]
        vth = vt_ref[h * DIFF_VDIM:(h + 1) * DIFF_VDIM, :]
        for c in range(2):
            st = jnp.dot(kh, qd_ref[h, :, c * tq:(c + 1) * tq], preferred_element_type=F32)
            unit(2 * h + c, st, vth, accd_ref, (h, c))
    kb = k_ref[:, DIFF_QK_W:EVEN_K_W]
    for n in range(GQA_KV_HEADS):
        vtn = vt_ref[DIFF_V_W + n * HEAD_DIM:DIFF_V_W + (n + 1) * HEAD_DIM, :]
        for g in range(GQA_GROUP):
            st = jnp.dot(kb, qb_ref[n, :, g * tq:(g + 1) * tq], preferred_element_type=F32)
            unit(2 * DIFF_HEADS + n * GQA_GROUP + g, st, vtn, accb_ref, n * GQA_GROUP + g)

    @pl.when(kv == pl.num_programs(2) - 1)
    def _finalize():
        lv = lam_ref[...]
        lam = (jnp.exp(jnp.sum(lv[0:1] * lv[1:2], axis=-1, keepdims=True))
               - jnp.exp(jnp.sum(lv[2:3] * lv[3:4], axis=-1, keepdims=True)) + lambda_init)
        for h in range(DIFF_HEADS):
            inv0 = 1.0 / l_ref[2 * h:2 * h + 1, :]
            inv1 = 1.0 / l_ref[2 * h + 1:2 * h + 2, :]
            ot = accd_ref[h, 0] * inv0 - lam * (accd_ref[h, 1] * inv1)
            o = ot.T
            ms = jnp.mean(o * o, axis=-1, keepdims=True)
            o = o * lax.rsqrt(ms + SUBLN_EPS) * sg_ref[...] * (1.0 - lambda_init)
            o_ref[:, h * DIFF_VDIM:(h + 1) * DIFF_VDIM] = o.astype(o_ref.dtype)
        for j in range(GQA_HEADS // 2):
            u = 2 * DIFF_HEADS + 2 * j
            o2 = jnp.concatenate([accb_ref[2 * j] * (1.0 / l_ref[u:u + 1, :]),
                                  accb_ref[2 * j + 1] * (1.0 / l_ref[u + 1:u + 2, :])], axis=0)
            c0 = DIFF_V_W + 2 * j * HEAD_DIM
            o_ref[:, c0:c0 + 2 * HEAD_DIM] = o2.T.astype(o_ref.dtype)


def _even_attn(qt, k, vt, lam_vec, subln_g, lambda_init, tq, tk):
    b, _, s = qt.shape
    kern = functools.partial(_even_attn_kernel, tq=tq, lambda_init=lambda_init)
    return pl.pallas_call(
        kern,
        out_shape=jax.ShapeDtypeStruct((b, s, MIX_W), BF16),
        grid=(b, s // tq, s // tk),
        in_specs=[
            pl.BlockSpec((None, EVEN_Q_W, tq), lambda bi, i, j: (bi, 0, i)),
            pl.BlockSpec((None, tk, EVEN_K_W), lambda bi, i, j: (bi, j, 0)),
            pl.BlockSpec((None, EVEN_V_W, tk), lambda bi, i, j: (bi, 0, j)),
            pl.BlockSpec((4, HEAD_DIM), lambda bi, i, j: (0, 0)),
            pl.BlockSpec((1, DIFF_VDIM), lambda bi, i, j: (0, 0)),
        ],
        out_specs=pl.BlockSpec((None, tq, MIX_W), lambda bi, i, j: (bi, i, 0)),
        scratch_shapes=[
            pltpu.VMEM((DIFF_HEADS, 2 * HEAD_DIM, 2 * tq), BF16),
            pltpu.VMEM((GQA_KV_HEADS, 2 * HEAD_DIM, GQA_GROUP * tq), BF16),
            pltpu.VMEM((N_UNITS, tq), F32),
            pltpu.VMEM((N_UNITS, tq), F32),
            pltpu.VMEM((DIFF_HEADS, 2, DIFF_VDIM, tq), F32),
            pltpu.VMEM((GQA_HEADS, HEAD_DIM, tq), F32),
        ],
        compiler_params=_params(("parallel", "parallel", "arbitrary")),
        name="even_attn",
    )(qt, k, vt, lam_vec, subln_g)


def _out_proj_kernel(x_ref, o_ref, w_ref, gate_ref, y_ref):
    m = jnp.dot(o_ref[...], w_ref[...], preferred_element_type=F32)
    y_ref[...] = x_ref[...] + gate_ref[...] * m


def _out_proj(x, o, w, gate, tm):
    b, s, d = x.shape
    kdim = o.shape[-1]
    return pl.pallas_call(
        _out_proj_kernel,
        out_shape=jax.ShapeDtypeStruct((b, s, d), F32),
        grid=(b, s // tm),
        in_specs=[
            pl.BlockSpec((None, tm, d), lambda bi, i: (bi, i, 0)),
            pl.BlockSpec((None, tm, kdim), lambda bi, i: (bi, i, 0)),
            pl.BlockSpec((kdim, d), lambda bi, i: (0, 0)),
            pl.BlockSpec((None, 1, d), lambda bi, i: (bi, 0, 0)),
        ],
        out_specs=pl.BlockSpec((None, tm, d), lambda bi, i: (bi, i, 0)),
        compiler_params=_params(("parallel", "parallel")),
        name="out_proj",
    )(x, o, w, gate)


def _odd_proj_kernel(x_ref, sc_ref, sh_ref, g_ref, w_ref, q_ref, k_ref, v_ref):
    h = _norm_mod(x_ref[...], g_ref[...], 1.0 + sc_ref[...], sh_ref[...]).astype(BF16)
    d = q_ref.shape[-1]
    q_ref[...] = (jnp.dot(h, w_ref[:, 0:d], preferred_element_type=F32) * Q_SCALE).astype(BF16)
    k_ref[...] = jnp.dot(h, w_ref[:, d:2 * d], preferred_element_type=F32).astype(BF16)
    v_ref[...] = jnp.dot(h, w_ref[:, 2 * d:3 * d], preferred_element_type=F32).astype(BF16)


def _odd_proj(x, sc, sh, g, w, tm):
    b, s, d = x.shape
    row = lambda bi, i: (bi, 0, 0)
    tok = pl.BlockSpec((None, tm, d), lambda bi, i: (bi, i, 0))
    shp = jax.ShapeDtypeStruct((b, s, d), BF16)
    return pl.pallas_call(
        _odd_proj_kernel,
        out_shape=(shp, shp, shp),
        grid=(b, s // tm),
        in_specs=[
            tok,
            pl.BlockSpec((None, 1, d), row),
            pl.BlockSpec((None, 1, d), row),
            pl.BlockSpec((1, d), lambda bi, i: (0, 0)),
            pl.BlockSpec((d, 3 * d), lambda bi, i: (0, 0)),
        ],
        out_specs=(tok, tok, tok),
        compiler_params=_params(("parallel", "parallel")),
        name="odd_proj",
    )(x, sc, sh, g, w)


NA_ROWS_PER_STEP = 8
NA_BLK = NA_ROWS_PER_STEP * GRID_W
NA_KEYS = NA_KH * GRID_W


def _na_kernel(q_ref, kp_ref, kc_ref, kn_ref, vp_ref, vc_ref, vn_ref, tb_ref, o_ref,
               kbuf, vbuf, *, rows):
    i = pl.program_id(1)
    kbuf[0:NA_BLK, :] = kp_ref[...]
    kbuf[NA_BLK:2 * NA_BLK, :] = kc_ref[...]
    kbuf[2 * NA_BLK:3 * NA_BLK, :] = kn_ref[...]
    vbuf[0:NA_BLK, :] = vp_ref[...]
    vbuf[NA_BLK:2 * NA_BLK, :] = vc_ref[...]
    vbuf[2 * NA_BLK:3 * NA_BLK, :] = vn_ref[...]
    r0 = i * NA_ROWS_PER_STEP
    left = lax.broadcasted_iota(jnp.int32, (GRID_W, 2 * HEAD_DIM), 1) < HEAD_DIM

    def row_body(j, carry):
        r = r0 + j
        rs = jnp.clip(r - NA_KH // 2, 0, rows - NA_KH)
        start = pl.multiple_of((rs - r0 + NA_ROWS_PER_STEP) * GRID_W, GRID_W)
        d0 = rs - r + (NA_KH - 1)
        qoff = pl.multiple_of(j * GRID_W, GRID_W)
        for pr in range(NA_HEADS // 2):
            cols = slice(pr * 2 * HEAD_DIM, (pr + 1) * 2 * HEAD_DIM)
            q2 = q_ref[pl.ds(qoff, GRID_W), cols]
            zero = jnp.zeros_like(q2)
            lhs = jnp.concatenate([jnp.where(left, q2, zero), jnp.where(left, zero, q2)], axis=0)
            k2 = kbuf[pl.ds(start, NA_KEYS), cols]
            v2 = vbuf[pl.ds(start, NA_KEYS), cols]
            sc = lax.dot_general(lhs, k2, NT_DIMS, preferred_element_type=F32)
            bias = jnp.concatenate(
                [jnp.concatenate([tb_ref[2 * pr + e, d0 + 2 * t] for t in range(NA_KH // 2)], axis=1)
                 for e in range(2)], axis=0)
            sc = sc + bias
            m = jnp.max(sc, axis=-1, keepdims=True)
            p = jnp.exp2(sc - m)
            l = jnp.sum(p, axis=-1, keepdims=True)
            res = jnp.dot(p.astype(BF16), v2, preferred_element_type=F32) * (1.0 / l)
            o2 = jnp.where(left, res[0:GRID_W], res[GRID_W:2 * GRID_W])
            o_ref[pl.ds(qoff, GRID_W), cols] = o2.astype(o_ref.dtype)
        return carry

    lax.fori_loop(0, NA_ROWS_PER_STEP, row_body, 0)


def _na_attn(q, k, v, tb2):
    b, s, d = q.shape
    rows = s // GRID_W
    nblk = rows // NA_ROWS_PER_STEP
    kern = functools.partial(_na_kernel, rows=rows)
    cur = pl.BlockSpec((None, NA_BLK, d), lambda bi, i: (bi, i, 0))
    prev = pl.BlockSpec((None, NA_BLK, d), lambda bi, i: (bi, jnp.maximum(i - 1, 0), 0))
    nxt = pl.BlockSpec((None, NA_BLK, d), lambda bi, i: (bi, jnp.minimum(i + 1, nblk - 1), 0))
    return pl.pallas_call(
        kern,
        out_shape=jax.ShapeDtypeStruct((b, s, d), BF16),
        grid=(b, nblk),
        in_specs=[cur, prev, cur, nxt, prev, cur, nxt,
                  pl.BlockSpec(tb2.shape, lambda bi, i: (0, 0, 0, 0))],
        out_specs=cur,
        scratch_shapes=[pltpu.VMEM((3 * NA_BLK, d), BF16), pltpu.VMEM((3 * NA_BLK, d), BF16)],
        compiler_params=_params(("parallel", "parallel")),
        name="na_attn",
    )(q, k, k, k, v, v, v, tb2)


def _na_bias_table(rpb):
    qc = jnp.arange(GRID_W, dtype=jnp.int32)[:, None]
    kc = jnp.arange(GRID_W, dtype=jnp.int32)[None, :]
    cs = jnp.clip(qc - NA_KW // 2, 0, GRID_W - NA_KW)
    valid = (kc >= cs) & (kc < cs + NA_KW)
    dc = jnp.clip(kc - qc + (NA_KW - 1), 0, 2 * NA_KW - 2)
    t = jnp.where(valid[None, None], rpb[:, :, dc].astype(F32) * LOG2E, NEG_BIG)
    return jnp.concatenate([t[:, :-1], t[:, 1:]], axis=-1)


FFN_HALO = SUBLANES


def _ffn_kernel(xp_ref, x_ref, xn_ref, sc_ref, sh_ref, gate_ref, g_ref, wg_ref, wv_ref,
                cwg_ref, cwv_ref, cbg_ref, cbv_ref, wd_ref, fg_ref, y_ref,
                h_ref, ug_ref, uv_ref, acc_ref, *, tm, final_norm):
    i = pl.program_id(1)
    last = pl.num_programs(1) - 1
    xe = jnp.concatenate([xp_ref[...], x_ref[...], xn_ref[...]], axis=0)
    h = _norm_mod(xe, g_ref[...], 1.0 + sc_ref[...], sh_ref[...])
    row = lax.broadcasted_iota(jnp.int32, (tm + 2 * FFN_HALO, 1), 0)
    inside = ((row >= FFN_HALO) | (i > 0)) & ((row < tm + FFN_HALO) | (i < last))
    h_ref[...] = jnp.where(inside, h, 0.0).astype(BF16)
    acc_ref[...] = jnp.zeros(acc_ref.shape, F32)

    def conv(u_ref, cw, cb):
        return (u_ref[FFN_HALO - 1:FFN_HALO - 1 + tm, :] * cw[0:1]
                + u_ref[FFN_HALO:FFN_HALO + tm, :] * cw[1:2]
                + u_ref[FFN_HALO + 1:FFN_HALO + 1 + tm, :] * cw[2:3] + cb)

    def chunk(c, carry):
        hb = h_ref[...]
        ug_ref[...] = jnp.dot(hb, wg_ref[c], preferred_element_type=F32)
        uv_ref[...] = jnp.dot(hb, wv_ref[c], preferred_element_type=F32)
        gt = conv(ug_ref, cwg_ref[c], cbg_ref[c])
        vl = conv(uv_ref, cwv_ref[c], cbv_ref[c])
        act = (gt * (1.0 / (1.0 + jnp.exp(-gt))) * vl).astype(BF16)
        acc_ref[...] += jnp.dot(act, wd_ref[c], preferred_element_type=F32)
        return carry

    lax.fori_loop(0, wg_ref.shape[0], chunk, 0)
    y = x_ref[...] + gate_ref[...] * acc_ref[...]
    if final_norm:
        ms = jnp.mean(y * y, axis=-1, keepdims=True)
        y = y * lax.rsqrt(ms + EPS) * fg_ref[...]
    y_ref[...] = y


def _ffn(x, sc, sh, gate, g, ffn_w, final_g, final_norm, tm):
    wg, wv, cwg, cwv, cbg, cbv, wd = ffn_w
    b, s, d = x.shape
    nc, _, tn = wg.shape
    hb = tm // FFN_HALO
    nh = s // FFN_HALO
    kern = functools.partial(_ffn_kernel, tm=tm, final_norm=final_norm)
    row = lambda bi, i: (bi, 0, 0)
    c3 = lambda bi, i: (0, 0, 0)
    return pl.pallas_call(
        kern,
        out_shape=jax.ShapeDtypeStruct((b, s, d), F32),
        grid=(b, s // tm),
        in_specs=[
            pl.BlockSpec((None, FFN_HALO, d), lambda bi, i: (bi, jnp.maximum(i * hb - 1, 0), 0)),
            pl.BlockSpec((None, tm, d), lambda bi, i: (bi, i, 0)),
            pl.BlockSpec((None, FFN_HALO, d), lambda bi, i: (bi, jnp.minimum((i + 1) * hb, nh - 1), 0)),
            pl.BlockSpec((None, 1, d), row),
            pl.BlockSpec((None, 1, d), row),
            pl.BlockSpec((None, 1, d), row),
            pl.BlockSpec((1, d), lambda bi, i: (0, 0)),
            pl.BlockSpec(wg.shape, c3),
            pl.BlockSpec(wv.shape, c3),
            pl.BlockSpec(cwg.shape, c3),
            pl.BlockSpec(cwv.shape, c3),
            pl.BlockSpec(cbg.shape, c3),
            pl.BlockSpec(cbv.shape, c3),
            pl.BlockSpec(wd.shape, c3),
            pl.BlockSpec((1, d), lambda bi, i: (0, 0)),
        ],
        out_specs=pl.BlockSpec((None, tm, d), lambda bi, i: (bi, i, 0)),
        scratch_shapes=[
            pltpu.VMEM((tm + 2 * FFN_HALO, d), BF16),
            pltpu.VMEM((tm + 2 * FFN_HALO, tn), F32),
            pltpu.VMEM((tm + 2 * FFN_HALO, tn), F32),
            pltpu.VMEM((tm, d), F32),
        ],
        compiler_params=_params(("parallel", "parallel")),
        name="conv_ffn",
    )(x, x, x, sc, sh, gate, g, wg, wv, cwg, cwv, cbg, cbv, wd, final_g)


def _ffn_weights(w_up, conv_w, conv_b, w_down, tn):
    d = w_up.shape[0]
    nc = D_FF // tn

    def cols(a):
        return a.reshape(a.shape[0], nc, tn).transpose(1, 0, 2)

    wg = cols(w_up[:, :D_FF]).astype(BF16)
    wv = cols(w_up[:, D_FF:]).astype(BF16)
    cwg = cols(conv_w[:, :D_FF])
    cwv = cols(conv_w[:, D_FF:])
    cbg = cols(conv_b[None, :D_FF])
    cbv = cols(conv_b[None, D_FF:])
    wd = w_down.reshape(nc, tn, d).astype(BF16)
    return wg, wv, cwg, cwv, cbg, cbv, wd


def _rope_tables(s):
    inv1 = ROPE_THETA ** (-jnp.arange(0, HEAD_DIM, 2, dtype=F32) / HEAD_DIM)
    t = jnp.arange(s, dtype=F32)
    ang1 = t[:, None] * inv1[None, :]
    inv2 = ROPE_THETA ** (-jnp.arange(0, HALF, 2, dtype=F32) / HALF)
    ti = jnp.arange(s, dtype=jnp.int32)
    rowf = (ti // GRID_W).astype(F32)
    colf = (ti % GRID_W).astype(F32)
    ang2 = jnp.concatenate([rowf[:, None] * inv2[None, :], colf[:, None] * inv2[None, :]], axis=-1)
    return jnp.cos(ang1).T, jnp.sin(ang1).T, jnp.cos(ang2).T, jnp.sin(ang2).T


def _tile(s, pref):
    t = min(pref, s)
    assert s % t == 0
    return t


def _trunk(x, c, p):
    b, s, d = x.shape
    depth = p["ada_w"].shape[0]
    tm = _tile(s, 512)
    tq = _tile(s, 512)
    tk = _tile(s, 512)
    mod = _ada_mod(c, p["ada_w"], p["ada_b"])
    rope = _rope_tables(s)
    for l in range(depth):
        sh1, sc1, g1, sh2, sc2, g2 = [mod[l, :, None, k * d:(k + 1) * d] for k in range(6)]
        ng = p["norm_g"][l]
        if l % 2 == 0:
            e = l // 2
            lambda_init = 0.8 - 0.6 * math.exp(-0.3 * l)
            qt, k, vt = _even_proj(x, sc1, sh1, ng[0:1], p["even_w_in_t"][e], rope,
                                   p["gqa_qk_g"][e, 0][:, None], p["gqa_qk_g"][e, 1][:, None], tm)
            o = _even_attn(qt, k, vt, p["diff_lambda"][e], p["diff_subln_g"][e][None, :],
                           lambda_init, tq, tk)
            w_out = p["even_w_out"][e]
        else:
            j = l // 2
            q, k, v = _odd_proj(x, sc1, sh1, ng[0:1], p["odd_w_qkv"][j], tm)
            o = _na_attn(q, k, v, p["odd_tb2"][j])
            w_out = p["odd_w_out"][j]
        x = _out_proj(x, o, w_out, g1, tm)
        x = _ffn(x, sc2, sh2, g2, ng[1:2], p["ffn"][l], p["final_g"], l == depth - 1, tm)
    return x


def kernel(x_prompt, x_sample, c_prompt, c_sample, ada_w, ada_b, norm_g, even_w_in, even_w_out, diff_lambda, diff_subln_g, gqa_qk_g, odd_w_qkv, odd_rpb, odd_w_out, ffn_w_up, ffn_conv_w, ffn_conv_b, ffn_w_down, final_g):
    depth = ada_w.shape[0]
    p = {
        "ada_w": ada_w, "ada_b": ada_b, "norm_g": norm_g,
        "even_w_in_t": jnp.swapaxes(even_w_in, 1, 2).astype(BF16),
        "even_w_out": even_w_out.astype(BF16),
        "diff_lambda": diff_lambda, "diff_subln_g": diff_subln_g, "gqa_qk_g": gqa_qk_g,
        "odd_w_qkv": odd_w_qkv.astype(BF16),
        "odd_tb2": [_na_bias_table(odd_rpb[j]) for j in range(odd_rpb.shape[0])],
        "odd_w_out": odd_w_out.astype(BF16),
        "ffn": [_ffn_weights(ffn_w_up[l], ffn_conv_w[l], ffn_conv_b[l], ffn_w_down[l], 256)
                for l in range(depth)],
        "final_g": final_g[None, :],
    }
    return (_trunk(x_prompt, c_prompt, p), _trunk(x_sample, c_sample, p))
```

```python
import functools
import math

import jax
import jax.numpy as jnp
from jax import lax
from jax.experimental import pallas as pl
from jax.experimental.pallas import tpu as pltpu

D_MODEL = 1024
GRID_W = 64
HEAD_DIM = 64
HALF = HEAD_DIM // 2
DIFF_HEADS = 4
DIFF_VDIM = 2 * HEAD_DIM
GQA_HEADS = 8
GQA_KV_HEADS = 2
GQA_GROUP = GQA_HEADS // GQA_KV_HEADS
NA_HEADS = D_MODEL // HEAD_DIM
NA_KH = 8
NA_KW = 16
DIFF_QK_W = DIFF_HEADS * 2 * HEAD_DIM
DIFF_V_W = DIFF_HEADS * DIFF_VDIM
GQA_Q_W = GQA_HEADS * HEAD_DIM
GQA_KV_W = GQA_KV_HEADS * HEAD_DIM
EVEN_IN_W = 2 * DIFF_QK_W + DIFF_V_W + GQA_Q_W + 2 * GQA_KV_W
EVEN_K_W = DIFF_QK_W + GQA_KV_W
EVEN_V_W = DIFF_V_W + GQA_KV_W
EVEN_Q_W = DIFF_QK_W + GQA_Q_W
MIX_W = DIFF_V_W + GQA_Q_W
D_FF = 2816
ROPE_THETA = 10000.0
EPS = 1e-6
SUBLN_EPS = 1e-5
LOG2E = math.log2(math.e)
Q_SCALE = HEAD_DIM ** -0.5 * LOG2E
NEG_BIG = -1e30
N_UNITS = 2 * DIFF_HEADS + GQA_HEADS

SUBLANES = 8
LANES = 128
EVEN_BLOCKS = DIFF_HEADS + 1
V7X_VMEM_LIMIT = 56 * 1024 * 1024
KEY_CHUNK = 256

F32 = jnp.float32
BF16 = jnp.bfloat16
NT_DIMS = (((1,), (1,)), ((), ()))


def _params(sem, vmem=V7X_VMEM_LIMIT):
    return pltpu.CompilerParams(dimension_semantics=sem, vmem_limit_bytes=vmem)


def _norm_mod(x, g, scale1p, shift):
    ms = jnp.mean(x * x, axis=-1, keepdims=True)
    return (x * lax.rsqrt(ms + EPS) * g) * scale1p + shift


def _ada_kernel(c_ref, w_ref, b_ref, o_ref):
    c = c_ref[...]
    act = (c * (1.0 / (1.0 + jnp.exp(-c)))).astype(BF16)
    o_ref[...] = jnp.dot(act, w_ref[...].astype(BF16), preferred_element_type=F32) + b_ref[...]


def _ada_mod(c, ada_w, ada_b):
    depth, d, n = ada_w.shape
    b = c.shape[0]
    bp = -(-b // SUBLANES) * SUBLANES
    cp = jnp.pad(c, ((0, bp - b), (0, 0)))
    tn = 1536
    out = pl.pallas_call(
        _ada_kernel,
        out_shape=jax.ShapeDtypeStruct((depth, bp, n), F32),
        grid=(depth, n // tn),
        in_specs=[
            pl.BlockSpec((bp, d), lambda l, j: (0, 0)),
            pl.BlockSpec((None, d, tn), lambda l, j: (l, 0, j)),
            pl.BlockSpec((None, 1, tn), lambda l, j: (l, 0, j)),
        ],
        out_specs=pl.BlockSpec((None, bp, tn), lambda l, j: (l, 0, j)),
        compiler_params=_params(("parallel", "parallel")),
        name="ada_mod",
    )(cp, ada_w, ada_b.reshape(depth, 1, n))
    return out[:, :b]


def _rope_fm(blk, cos, sin):
    x1 = blk[0:HALF]
    x2 = blk[HALF:HEAD_DIM]
    return jnp.concatenate([x1 * cos - x2 * sin, x2 * cos + x1 * sin], axis=0)


def _rms_fm(blk, g):
    ms = jnp.mean(blk * blk, axis=0, keepdims=True)
    return blk * lax.rsqrt(ms + EPS) * g


def _even_proj_kernel(x_ref, sc_ref, sh_ref, g_ref, wt_ref, c1_ref, s1_ref, c2_ref, s2_ref,
                      gq_ref, gk_ref, qt_ref, k_ref, vt_ref, pt_ref, kt_ref):
    h = _norm_mod(x_ref[...], g_ref[...], 1.0 + sc_ref[...], sh_ref[...]).astype(BF16)
    pt_ref[...] = lax.dot_general(wt_ref[...], h, NT_DIMS, preferred_element_type=F32)
    c1, s1, c2, s2 = c1_ref[...], s1_ref[...], c2_ref[...], s2_ref[...]
    gq, gk = gq_ref[...], gk_ref[...]
    qa0, ka0, va0 = 0, DIFF_QK_W, 2 * DIFF_QK_W
    qb0 = va0 + DIFF_V_W
    kb0 = qb0 + GQA_Q_W
    vb0 = kb0 + GQA_KV_W
    for j in range(DIFF_QK_W // HEAD_DIM):
        r = j * HEAD_DIM
        q = _rope_fm(pt_ref[qa0 + r:qa0 + r + HEAD_DIM, :], c1, s1) * Q_SCALE
        qt_ref[r:r + HEAD_DIM, :] = q.astype(BF16)
        kt_ref[r:r + HEAD_DIM, :] = _rope_fm(pt_ref[ka0 + r:ka0 + r + HEAD_DIM, :], c1, s1)
    for j in range(GQA_HEADS):
        r = j * HEAD_DIM
        q = _rope_fm(_rms_fm(pt_ref[qb0 + r:qb0 + r + HEAD_DIM, :], gq), c2, s2) * Q_SCALE
        qt_ref[DIFF_QK_W + r:DIFF_QK_W + r + HEAD_DIM, :] = q.astype(BF16)
    for j in range(GQA_KV_HEADS):
        r = j * HEAD_DIM
        kt_ref[DIFF_QK_W + r:DIFF_QK_W + r + HEAD_DIM, :] = _rope_fm(
            _rms_fm(pt_ref[kb0 + r:kb0 + r + HEAD_DIM, :], gk), c2, s2)
    for blk in range(EVEN_BLOCKS):
        rows = slice(blk * LANES, (blk + 1) * LANES)
        k_ref[blk] = kt_ref[rows, :].T.astype(BF16)
        v0 = va0 if blk < DIFF_HEADS else vb0 - DIFF_V_W
        vt_ref[blk] = pt_ref[v0 + blk * LANES:v0 + (blk + 1) * LANES, :].astype(BF16)


def _even_proj(x, sc, sh, g, w_in_t, rope, gq, gk, tm):
    b, s, d = x.shape
    c1, s1, c2, s2 = rope
    row = lambda bi, i: (bi, 0, 0)
    tab = pl.BlockSpec((HALF, tm), lambda bi, i: (0, i))
    full2 = lambda shp: pl.BlockSpec(shp, lambda bi, i: (0, 0))
    return pl.pallas_call(
        _even_proj_kernel,
        out_shape=(jax.ShapeDtypeStruct((b, EVEN_Q_W, s), BF16),
                   jax.ShapeDtypeStruct((b, EVEN_BLOCKS, s, LANES), BF16),
                   jax.ShapeDtypeStruct((b, EVEN_BLOCKS, LANES, s), BF16)),
        grid=(b, s // tm),
        in_specs=[
            pl.BlockSpec((None, tm, d), lambda bi, i: (bi, i, 0)),
            pl.BlockSpec((None, 1, d), row),
            pl.BlockSpec((None, 1, d), row),
            full2((1, d)),
            full2((EVEN_IN_W, d)),
            tab, tab, tab, tab,
            full2((HEAD_DIM, 1)),
            full2((HEAD_DIM, 1)),
        ],
        out_specs=(pl.BlockSpec((None, EVEN_Q_W, tm), lambda bi, i: (bi, 0, i)),
                   pl.BlockSpec((None, EVEN_BLOCKS, tm, LANES), lambda bi, i: (bi, 0, i, 0)),
                   pl.BlockSpec((None, EVEN_BLOCKS, LANES, tm), lambda bi, i: (bi, 0, 0, i))),
        scratch_shapes=[pltpu.VMEM((EVEN_IN_W, tm), F32), pltpu.VMEM((EVEN_K_W, tm), F32)],
        compiler_params=_params(("parallel", "parallel")),
        name="even_proj",
    )(x, sc, sh, g, w_in_t, c1, s1, c2, s2, gq, gk)


def _even_attn_kernel(qt_ref, k_ref, vt_ref, lam_ref, sg_ref, o_ref,
                      rhs_ref, m_ref, l_ref, acc_ref, s0_ref, s1_ref, st0_ref, st1_ref,
                      *, tq, tk, lambda_init):
    kv = pl.program_id(2)

    @pl.when(kv == 0)
    def _init():
        top = lax.broadcasted_iota(jnp.int32, (LANES, tq), 0) < HEAD_DIM
        zero = jnp.zeros((LANES, tq), BF16)
        for h in range(DIFF_HEADS):
            qh = qt_ref[h * LANES:(h + 1) * LANES, :]
            rhs_ref[2 * h] = jnp.where(top, qh, zero)
            rhs_ref[2 * h + 1] = jnp.where(top, zero, qh)
        for n in range(GQA_KV_HEADS):
            for g in range(GQA_GROUP):
                j = n * GQA_GROUP + g
                r = DIFF_QK_W + j * HEAD_DIM
                qg = jnp.concatenate([qt_ref[r:r + HEAD_DIM, :]] * 2, axis=0)
                keep = top if n == 0 else jnp.logical_not(top)
                rhs_ref[2 * DIFF_HEADS + j] = jnp.where(keep, qg, zero)
        m_ref[...] = jnp.full(m_ref.shape, NEG_BIG, F32)
        l_ref[...] = jnp.zeros(l_ref.shape, F32)
        acc_ref[...] = jnp.zeros(acc_ref.shape, F32)

    def scores(u, s_ref, st_ref):
        blk = jnp.minimum(u >> 1, EVEN_BLOCKS - 1)
        rhs = rhs_ref[u]
        m8 = None
        for r in range(tk // KEY_CHUNK):
            rows = slice(r * KEY_CHUNK, (r + 1) * KEY_CHUNK)
            st = jnp.dot(k_ref[blk, rows, :], rhs, preferred_element_type=F32)
            s_ref[rows, :] = st
            c8 = jnp.max(st.reshape(KEY_CHUNK // SUBLANES, SUBLANES, tq), axis=0)
            m8 = c8 if m8 is None else jnp.maximum(m8, c8)
        m_old = m_ref[u]
        m_new = jnp.maximum(m_old, jnp.max(m8, axis=0, keepdims=True))
        m_ref[u] = m_new
        st_ref[0] = m_new
        st_ref[1] = jnp.exp2(m_old - m_new)

    def update(u, s_ref, st_ref):
        blk = jnp.minimum(u >> 1, EVEN_BLOCKS - 1)
        m_new = st_ref[0]
        alpha = st_ref[1]
        l8 = None
        pv = None
        for r in range(tk // KEY_CHUNK):
            rows = slice(r * KEY_CHUNK, (r + 1) * KEY_CHUNK)
            p = jnp.exp2(s_ref[rows, :] - m_new)
            c8 = jnp.sum(p.reshape(KEY_CHUNK // SUBLANES, SUBLANES, tq), axis=0)
            l8 = c8 if l8 is None else l8 + c8
            d = jnp.dot(vt_ref[blk, :, rows], p.astype(BF16), preferred_element_type=F32)
            pv = d if pv is None else pv + d
        l_ref[u] = alpha * l_ref[u] + jnp.sum(l8, axis=0, keepdims=True)
        acc_ref[u] = alpha * acc_ref[u] + pv

    scores(0, s0_ref, st0_ref)

    def pair(j, carry):
        u = 2 * j
        scores(u + 1, s1_ref, st1_ref)
        update(u, s0_ref, st0_ref)
        scores(u + 2, s0_ref, st0_ref)
        update(u + 1, s1_ref, st1_ref)
        return carry

    lax.fori_loop(0, N_UNITS // 2 - 1, pair, 0)
    update(N_UNITS - 2, s0_ref, st0_ref)
    scores(N_UNITS - 1, s1_ref, st1_ref)
    update(N_UNITS - 1, s1_ref, st1_ref)

    @pl.when(kv == pl.num_programs(2) - 1)
    def _finalize():
        lv = lam_ref[...]
        lam = (jnp.exp(jnp.sum(lv[0:1] * lv[1:2], axis=-1, keepdims=True))
               - jnp.exp(jnp.sum(lv[2:3] * lv[3:4], axis=-1, keepdims=True)) + lambda_init)
        for h in range(DIFF_HEADS):
            inv0 = 1.0 / l_ref[2 * h]
            inv1 = 1.0 / l_ref[2 * h + 1]
            ot = acc_ref[2 * h] * inv0 - lam * (acc_ref[2 * h + 1] * inv1)
            o = ot.T
            ms = jnp.mean(o * o, axis=-1, keepdims=True)
            o = o * lax.rsqrt(ms + SUBLN_EPS) * sg_ref[...] * (1.0 - lambda_init)
            o_ref[:, h * DIFF_VDIM:(h + 1) * DIFF_VDIM] = o.astype(o_ref.dtype)
        for j in range(GQA_HEADS // 2):
            u = 2 * DIFF_HEADS + 2 * j
            rows = slice((u - 2 * DIFF_HEADS) // GQA_GROUP * HEAD_DIM,
                         ((u - 2 * DIFF_HEADS) // GQA_GROUP + 1) * HEAD_DIM)
            o2 = jnp.concatenate([acc_ref[u, rows, :] * (1.0 / l_ref[u]),
                                  acc_ref[u + 1, rows, :] * (1.0 / l_ref[u + 1])], axis=0)
            c0 = DIFF_V_W + 2 * j * HEAD_DIM
            o_ref[:, c0:c0 + 2 * HEAD_DIM] = o2.T.astype(o_ref.dtype)


def _even_attn(qt, k, vt, lam_vec, subln_g, lambda_init, tq, tk):
    b, _, s = qt.shape
    kern = functools.partial(_even_attn_kernel, tq=tq, tk=tk, lambda_init=lambda_init)
    return pl.pallas_call(
        kern,
        out_shape=jax.ShapeDtypeStruct((b, s, MIX_W), BF16),
        grid=(b, s // tq, s // tk),
        in_specs=[
            pl.BlockSpec((None, EVEN_Q_W, tq), lambda bi, i, j: (bi, 0, i)),
            pl.BlockSpec((None, EVEN_BLOCKS, tk, LANES), lambda bi, i, j: (bi, 0, j, 0)),
            pl.BlockSpec((None, EVEN_BLOCKS, LANES, tk), lambda bi, i, j: (bi, 0, 0, j)),
            pl.BlockSpec((4, HEAD_DIM), lambda bi, i, j: (0, 0)),
            pl.BlockSpec((1, DIFF_VDIM), lambda bi, i, j: (0, 0)),
        ],
        out_specs=pl.BlockSpec((None, tq, MIX_W), lambda bi, i, j: (bi, i, 0)),
        scratch_shapes=[
            pltpu.VMEM((N_UNITS, LANES, tq), BF16),
            pltpu.VMEM((N_UNITS, 1, tq), F32),
            pltpu.VMEM((N_UNITS, 1, tq), F32),
            pltpu.VMEM((N_UNITS, LANES, tq), F32),
            pltpu.VMEM((tk, tq), F32),
            pltpu.VMEM((tk, tq), F32),
            pltpu.VMEM((2, 1, tq), F32),
            pltpu.VMEM((2, 1, tq), F32),
        ],
        compiler_params=_params(("parallel", "parallel", "arbitrary")),
        name="even_attn",
    )(qt, k, vt, lam_vec, subln_g)


def _out_proj_kernel(x_ref, o_ref, w_ref, gate_ref, y_ref):
    m = jnp.dot(o_ref[...], w_ref[...], preferred_element_type=F32)
    y_ref[...] = x_ref[...] + gate_ref[...] * m


def _out_proj(x, o, w, gate, tm):
    b, s, d = x.shape
    kdim = o.shape[-1]
    return pl.pallas_call(
        _out_proj_kernel,
        out_shape=jax.ShapeDtypeStruct((b, s, d), F32),
        grid=(b, s // tm),
        in_specs=[
            pl.BlockSpec((None, tm, d), lambda bi, i: (bi, i, 0)),
            pl.BlockSpec((None, tm, kdim), lambda bi, i: (bi, i, 0)),
            pl.BlockSpec((kdim, d), lambda bi, i: (0, 0)),
            pl.BlockSpec((None, 1, d), lambda bi, i: (bi, 0, 0)),
        ],
        out_specs=pl.BlockSpec((None, tm, d), lambda bi, i: (bi, i, 0)),
        compiler_params=_params(("parallel", "parallel")),
        name="out_proj",
    )(x, o, w, gate)


def _odd_proj_kernel(x_ref, sc_ref, sh_ref, g_ref, w_ref, q_ref, k_ref, v_ref):
    h = _norm_mod(x_ref[...], g_ref[...], 1.0 + sc_ref[...], sh_ref[...]).astype(BF16)
    d = q_ref.shape[-1]
    q_ref[...] = (jnp.dot(h, w_ref[:, 0:d], preferred_element_type=F32) * Q_SCALE).astype(BF16)
    k_ref[...] = jnp.dot(h, w_ref[:, d:2 * d], preferred_element_type=F32).astype(BF16)
    v_ref[...] = jnp.dot(h, w_ref[:, 2 * d:3 * d], preferred_element_type=F32).astype(BF16)


def _odd_proj(x, sc, sh, g, w, tm):
    b, s, d = x.shape
    row = lambda bi, i: (bi, 0, 0)
    tok = pl.BlockSpec((None, tm, d), lambda bi, i: (bi, i, 0))
    shp = jax.ShapeDtypeStruct((b, s, d), BF16)
    return pl.pallas_call(
        _odd_proj_kernel,
        out_shape=(shp, shp, shp),
        grid=(b, s // tm),
        in_specs=[
            tok,
            pl.BlockSpec((None, 1, d), row),
            pl.BlockSpec((None, 1, d), row),
            pl.BlockSpec((1, d), lambda bi, i: (0, 0)),
            pl.BlockSpec((d, 3 * d), lambda bi, i: (0, 0)),
        ],
        out_specs=(tok, tok, tok),
        compiler_params=_params(("parallel", "parallel")),
        name="odd_proj",
    )(x, sc, sh, g, w)


NA_ROWS_PER_STEP = 8
NA_BLK = NA_ROWS_PER_STEP * GRID_W
NA_KEYS = NA_KH * GRID_W


def _na_kernel(q_ref, kp_ref, kc_ref, kn_ref, vp_ref, vc_ref, vn_ref, tb_ref, o_ref,
               kbuf, vbuf, *, rows):
    i = pl.program_id(1)
    kbuf[0:NA_BLK, :] = kp_ref[...]
    kbuf[NA_BLK:2 * NA_BLK, :] = kc_ref[...]
    kbuf[2 * NA_BLK:3 * NA_BLK, :] = kn_ref[...]
    vbuf[0:NA_BLK, :] = vp_ref[...]
    vbuf[NA_BLK:2 * NA_BLK, :] = vc_ref[...]
    vbuf[2 * NA_BLK:3 * NA_BLK, :] = vn_ref[...]
    r0 = i * NA_ROWS_PER_STEP
    left = lax.broadcasted_iota(jnp.int32, (GRID_W, 2 * HEAD_DIM), 1) < HEAD_DIM

    def row_body(j, carry):
        r = r0 + j
        rs = jnp.clip(r - NA_KH // 2, 0, rows - NA_KH)
        start = pl.multiple_of((rs - r0 + NA_ROWS_PER_STEP) * GRID_W, GRID_W)
        d0 = rs - r + (NA_KH - 1)
        qoff = pl.multiple_of(j * GRID_W, GRID_W)
        for pr in range(NA_HEADS // 2):
            cols = slice(pr * 2 * HEAD_DIM, (pr + 1) * 2 * HEAD_DIM)
            q2 = q_ref[pl.ds(qoff, GRID_W), cols]
            zero = jnp.zeros_like(q2)
            lhs = jnp.concatenate([jnp.where(left, q2, zero), jnp.where(left, zero, q2)], axis=0)
            k2 = kbuf[pl.ds(start, NA_KEYS), cols]
            v2 = vbuf[pl.ds(start, NA_KEYS), cols]
            sc = lax.dot_general(lhs, k2, NT_DIMS, preferred_element_type=F32)
            bias = jnp.concatenate(
                [jnp.concatenate([tb_ref[2 * pr + e, d0 + 2 * t] for t in range(NA_KH // 2)], axis=1)
                 for e in range(2)], axis=0)
            sc = sc + bias
            m = jnp.max(sc, axis=-1, keepdims=True)
            p = jnp.exp2(sc - m)
            l = jnp.sum(p, axis=-1, keepdims=True)
            res = jnp.dot(p.astype(BF16), v2, preferred_element_type=F32) * (1.0 / l)
            o2 = jnp.where(left, res[0:GRID_W], res[GRID_W:2 * GRID_W])
            o_ref[pl.ds(qoff, GRID_W), cols] = o2.astype(o_ref.dtype)
        return carry

    lax.fori_loop(0, NA_ROWS_PER_STEP, row_body, 0)


def _na_attn(q, k, v, tb2):
    b, s, d = q.shape
    rows = s // GRID_W
    nblk = rows // NA_ROWS_PER_STEP
    kern = functools.partial(_na_kernel, rows=rows)
    cur = pl.BlockSpec((None, NA_BLK, d), lambda bi, i: (bi, i, 0))
    prev = pl.BlockSpec((None, NA_BLK, d), lambda bi, i: (bi, jnp.maximum(i - 1, 0), 0))
    nxt = pl.BlockSpec((None, NA_BLK, d), lambda bi, i: (bi, jnp.minimum(i + 1, nblk - 1), 0))
    return pl.pallas_call(
        kern,
        out_shape=jax.ShapeDtypeStruct((b, s, d), BF16),
        grid=(b, nblk),
        in_specs=[cur, prev, cur, nxt, prev, cur, nxt,
                  pl.BlockSpec(tb2.shape, lambda bi, i: (0, 0, 0, 0))],
        out_specs=cur,
        scratch_shapes=[pltpu.VMEM((3 * NA_BLK, d), BF16), pltpu.VMEM((3 * NA_BLK, d), BF16)],
        compiler_params=_params(("parallel", "parallel")),
        name="na_attn",
    )(q, k, k, k, v, v, v, tb2)


def _na_bias_table(rpb):
    qc = jnp.arange(GRID_W, dtype=jnp.int32)[:, None]
    kc = jnp.arange(GRID_W, dtype=jnp.int32)[None, :]
    cs = jnp.clip(qc - NA_KW // 2, 0, GRID_W - NA_KW)
    valid = (kc >= cs) & (kc < cs + NA_KW)
    dc = jnp.clip(kc - qc + (NA_KW - 1), 0, 2 * NA_KW - 2)
    t = jnp.where(valid[None, None], rpb[:, :, dc].astype(F32) * LOG2E, NEG_BIG)
    return jnp.concatenate([t[:, :-1], t[:, 1:]], axis=-1)


FFN_HALO = SUBLANES


def _ffn_kernel(xp_ref, x_ref, xn_ref, sc_ref, sh_ref, gate_ref, g_ref, wg_ref, wv_ref,
                cwg_ref, cwv_ref, cbg_ref, cbv_ref, wd_ref, fg_ref, y_ref,
                h_ref, ug_ref, uv_ref, acc_ref, *, tm, final_norm):
    i = pl.program_id(1)
    last = pl.num_programs(1) - 1
    xe = jnp.concatenate([xp_ref[...], x_ref[...], xn_ref[...]], axis=0)
    h = _norm_mod(xe, g_ref[...], 1.0 + sc_ref[...], sh_ref[...])
    row = lax.broadcasted_iota(jnp.int32, (tm + 2 * FFN_HALO, 1), 0)
    inside = ((row >= FFN_HALO) | (i > 0)) & ((row < tm + FFN_HALO) | (i < last))
    h_ref[...] = jnp.where(inside, h, 0.0).astype(BF16)
    acc_ref[...] = jnp.zeros(acc_ref.shape, F32)

    def conv(u_ref, cw, cb):
        return (u_ref[FFN_HALO - 1:FFN_HALO - 1 + tm, :] * cw[0:1]
                + u_ref[FFN_HALO:FFN_HALO + tm, :] * cw[1:2]
                + u_ref[FFN_HALO + 1:FFN_HALO + 1 + tm, :] * cw[2:3] + cb)

    def chunk(c, carry):
        hb = h_ref[...]
        ug_ref[...] = jnp.dot(hb, wg_ref[c], preferred_element_type=F32)
        uv_ref[...] = jnp.dot(hb, wv_ref[c], preferred_element_type=F32)
        gt = conv(ug_ref, cwg_ref[c], cbg_ref[c])
        vl = conv(uv_ref, cwv_ref[c], cbv_ref[c])
        act = (gt * (1.0 / (1.0 + jnp.exp(-gt))) * vl).astype(BF16)
        acc_ref[...] += jnp.dot(act, wd_ref[c], preferred_element_type=F32)
        return carry

    lax.fori_loop(0, wg_ref.shape[0], chunk, 0)
    y = x_ref[...] + gate_ref[...] * acc_ref[...]
    if final_norm:
        ms = jnp.mean(y * y, axis=-1, keepdims=True)
        y = y * lax.rsqrt(ms + EPS) * fg_ref[...]
    y_ref[...] = y


def _ffn(x, sc, sh, gate, g, ffn_w, final_g, final_norm, tm):
    wg, wv, cwg, cwv, cbg, cbv, wd = ffn_w
    b, s, d = x.shape
    nc, _, tn = wg.shape
    hb = tm // FFN_HALO
    nh = s // FFN_HALO
    kern = functools.partial(_ffn_kernel, tm=tm, final_norm=final_norm)
    row = lambda bi, i: (bi, 0, 0)
    c3 = lambda bi, i: (0, 0, 0)
    return pl.pallas_call(
        kern,
        out_shape=jax.ShapeDtypeStruct((b, s, d), F32),
        grid=(b, s // tm),
        in_specs=[
            pl.BlockSpec((None, FFN_HALO, d), lambda bi, i: (bi, jnp.maximum(i * hb - 1, 0), 0)),
            pl.BlockSpec((None, tm, d), lambda bi, i: (bi, i, 0)),
            pl.BlockSpec((None, FFN_HALO, d), lambda bi, i: (bi, jnp.minimum((i + 1) * hb, nh - 1), 0)),
            pl.BlockSpec((None, 1, d), row),
            pl.BlockSpec((None, 1, d), row),
            pl.BlockSpec((None, 1, d), row),
            pl.BlockSpec((1, d), lambda bi, i: (0, 0)),
            pl.BlockSpec(wg.shape, c3),
            pl.BlockSpec(wv.shape, c3),
            pl.BlockSpec(cwg.shape, c3),
            pl.BlockSpec(cwv.shape, c3),
            pl.BlockSpec(cbg.shape, c3),
            pl.BlockSpec(cbv.shape, c3),
            pl.BlockSpec(wd.shape, c3),
            pl.BlockSpec((1, d), lambda bi, i: (0, 0)),
        ],
        out_specs=pl.BlockSpec((None, tm, d), lambda bi, i: (bi, i, 0)),
        scratch_shapes=[
            pltpu.VMEM((tm + 2 * FFN_HALO, d), BF16),
            pltpu.VMEM((tm + 2 * FFN_HALO, tn), F32),
            pltpu.VMEM((tm + 2 * FFN_HALO, tn), F32),
            pltpu.VMEM((tm, d), F32),
        ],
        compiler_params=_params(("parallel", "parallel")),
        name="conv_ffn",
    )(x, x, x, sc, sh, gate, g, wg, wv, cwg, cwv, cbg, cbv, wd, final_g)


def _ffn_weights(w_up, conv_w, conv_b, w_down, tn):
    d = w_up.shape[0]
    nc = D_FF // tn

    def cols(a):
        return a.reshape(a.shape[0], nc, tn).transpose(1, 0, 2)

    wg = cols(w_up[:, :D_FF]).astype(BF16)
    wv = cols(w_up[:, D_FF:]).astype(BF16)
    cwg = cols(conv_w[:, :D_FF])
    cwv = cols(conv_w[:, D_FF:])
    cbg = cols(conv_b[None, :D_FF])
    cbv = cols(conv_b[None, D_FF:])
    wd = w_down.reshape(nc, tn, d).astype(BF16)
    return wg, wv, cwg, cwv, cbg, cbv, wd


def _rope_tables(s):
    inv1 = ROPE_THETA ** (-jnp.arange(0, HEAD_DIM, 2, dtype=F32) / HEAD_DIM)
    t = jnp.arange(s, dtype=F32)
    ang1 = t[:, None] * inv1[None, :]
    inv2 = ROPE_THETA ** (-jnp.arange(0, HALF, 2, dtype=F32) / HALF)
    ti = jnp.arange(s, dtype=jnp.int32)
    rowf = (ti // GRID_W).astype(F32)
    colf = (ti % GRID_W).astype(F32)
    ang2 = jnp.concatenate([rowf[:, None] * inv2[None, :], colf[:, None] * inv2[None, :]], axis=-1)
    return jnp.cos(ang1).T, jnp.sin(ang1).T, jnp.cos(ang2).T, jnp.sin(ang2).T


def _tile(s, pref):
    t = min(pref, s)
    assert s % t == 0
    return t


def _trunk(x, c, p):
    b, s, d = x.shape
    depth = p["ada_w"].shape[0]
    tm = _tile(s, 512)
    tq = _tile(s, 512)
    tk = _tile(s, 1024)
    mod = _ada_mod(c, p["ada_w"], p["ada_b"])
    rope = _rope_tables(s)
    for l in range(depth):
        sh1, sc1, g1, sh2, sc2, g2 = [mod[l, :, None, k * d:(k + 1) * d] for k in range(6)]
        ng = p["norm_g"][l]
        if l % 2 == 0:
            e = l // 2
            lambda_init = 0.8 - 0.6 * math.exp(-0.3 * l)
            qt, k, vt = _even_proj(x, sc1, sh1, ng[0:1], p["even_w_in_t"][e], rope,
                                   p["gqa_qk_g"][e, 0][:, None], p["gqa_qk_g"][e, 1][:, None], tm)
            o = _even_attn(qt, k, vt, p["diff_lambda"][e], p["diff_subln_g"][e][None, :],
                           lambda_init, tq, tk)
            w_out = p["even_w_out"][e]
        else:
            j = l // 2
            q, k, v = _odd_proj(x, sc1, sh1, ng[0:1], p["odd_w_qkv"][j], tm)
            o = _na_attn(q, k, v, p["odd_tb2"][j])
            w_out = p["odd_w_out"][j]
        x = _out_proj(x, o, w_out, g1, tm)
        x = _ffn(x, sc2, sh2, g2, ng[1:2], p["ffn"][l], p["final_g"], l == depth - 1, tm)
    return x


def kernel(x_prompt, x_sample, c_prompt, c_sample, ada_w, ada_b, norm_g, even_w_in, even_w_out, diff_lambda, diff_subln_g, gqa_qk_g, odd_w_qkv, odd_rpb, odd_w_out, ffn_w_up, ffn_conv_w, ffn_conv_b, ffn_w_down, final_g):
    depth = ada_w.shape[0]
    p = {
        "ada_w": ada_w, "ada_b": ada_b, "norm_g": norm_g,
        "even_w_in_t": jnp.swapaxes(even_w_in, 1, 2).astype(BF16),
        "even_w_out": even_w_out.astype(BF16),
        "diff_lambda": diff_lambda, "diff_subln_g": diff_subln_g, "gqa_qk_g": gqa_qk_g,
        "odd_w_qkv": odd_w_qkv.astype(BF16),
        "odd_tb2": [_na_bias_table(odd_rpb[j]) for j in range(odd_rpb.shape[0])],
        "odd_w_out": odd_w_out.astype(BF16),
        "ffn": [_ffn_weights(ffn_w_up[l], ffn_conv_w[l], ffn_conv_b[l], ffn_w_down[l], 256)
                for l in range(depth)],
        "final_g": final_g[None, :],
    }
    return (_trunk(x_prompt, c_prompt, p), _trunk(x_sample, c_sample, p))
```

```python
import functools
import math

import jax
import jax.numpy as jnp
from jax import lax
from jax.experimental import pallas as pl
from jax.experimental.pallas import tpu as pltpu

D_MODEL = 1024
GRID_W = 64
HEAD_DIM = 64
HALF = HEAD_DIM // 2
DIFF_HEADS = 4
DIFF_VDIM = 2 * HEAD_DIM
GQA_HEADS = 8
GQA_KV_HEADS = 2
GQA_GROUP = GQA_HEADS // GQA_KV_HEADS
NA_HEADS = D_MODEL // HEAD_DIM
NA_KH = 8
NA_KW = 16
DIFF_QK_W = DIFF_HEADS * 2 * HEAD_DIM
DIFF_V_W = DIFF_HEADS * DIFF_VDIM
GQA_Q_W = GQA_HEADS * HEAD_DIM
GQA_KV_W = GQA_KV_HEADS * HEAD_DIM
EVEN_IN_W = 2 * DIFF_QK_W + DIFF_V_W + GQA_Q_W + 2 * GQA_KV_W
EVEN_K_W = DIFF_QK_W + GQA_KV_W
EVEN_V_W = DIFF_V_W + GQA_KV_W
EVEN_Q_W = DIFF_QK_W + GQA_Q_W
MIX_W = DIFF_V_W + GQA_Q_W
D_FF = 2816
ROPE_THETA = 10000.0
EPS = 1e-6
SUBLN_EPS = 1e-5
LOG2E = math.log2(math.e)
Q_SCALE = HEAD_DIM ** -0.5 * LOG2E
NEG_BIG = -1e30
N_UNITS = 2 * DIFF_HEADS + GQA_HEADS
UNIT_SHIFT = N_UNITS.bit_length() - 1
assert N_UNITS == 1 << UNIT_SHIFT

SUBLANES = 8
LANES = 128
EVEN_BLOCKS = DIFF_HEADS + 1
V7X_VMEM_LIMIT = 56 * 1024 * 1024
KEY_CHUNK = 256

F32 = jnp.float32
BF16 = jnp.bfloat16
NT_DIMS = (((1,), (1,)), ((), ()))


def _params(sem, vmem=V7X_VMEM_LIMIT):
    return pltpu.CompilerParams(dimension_semantics=sem, vmem_limit_bytes=vmem)


def _norm_mod(x, g, scale1p, shift):
    ms = jnp.mean(x * x, axis=-1, keepdims=True)
    return (x * lax.rsqrt(ms + EPS) * g) * scale1p + shift


def _ada_kernel(c_ref, w_ref, b_ref, o_ref):
    c = c_ref[...]
    act = (c * (1.0 / (1.0 + jnp.exp(-c)))).astype(BF16)
    o_ref[...] = jnp.dot(act, w_ref[...].astype(BF16), preferred_element_type=F32) + b_ref[...]


def _ada_mod(c, ada_w, ada_b):
    depth, d, n = ada_w.shape
    b = c.shape[0]
    bp = -(-b // SUBLANES) * SUBLANES
    cp = jnp.pad(c, ((0, bp - b), (0, 0)))
    tn = 1536
    out = pl.pallas_call(
        _ada_kernel,
        out_shape=jax.ShapeDtypeStruct((depth, bp, n), F32),
        grid=(depth, n // tn),
        in_specs=[
            pl.BlockSpec((bp, d), lambda l, j: (0, 0)),
            pl.BlockSpec((None, d, tn), lambda l, j: (l, 0, j)),
            pl.BlockSpec((None, 1, tn), lambda l, j: (l, 0, j)),
        ],
        out_specs=pl.BlockSpec((None, bp, tn), lambda l, j: (l, 0, j)),
        compiler_params=_params(("parallel", "parallel")),
        name="ada_mod",
    )(cp, ada_w, ada_b.reshape(depth, 1, n))
    return out[:, :b]


def _rope_fm(blk, cos, sin):
    x1 = blk[0:HALF]
    x2 = blk[HALF:HEAD_DIM]
    return jnp.concatenate([x1 * cos - x2 * sin, x2 * cos + x1 * sin], axis=0)


def _rms_fm(blk, g):
    ms = jnp.mean(blk * blk, axis=0, keepdims=True)
    return blk * lax.rsqrt(ms + EPS) * g


def _even_proj_kernel(x_ref, sc_ref, sh_ref, g_ref, wt_ref, c1_ref, s1_ref, c2_ref, s2_ref,
                      gq_ref, gk_ref, qt_ref, k_ref, vt_ref, pt_ref, kt_ref):
    h = _norm_mod(x_ref[...], g_ref[...], 1.0 + sc_ref[...], sh_ref[...]).astype(BF16)
    pt_ref[...] = lax.dot_general(wt_ref[...], h, NT_DIMS, preferred_element_type=F32)
    c1, s1, c2, s2 = c1_ref[...], s1_ref[...], c2_ref[...], s2_ref[...]
    gq, gk = gq_ref[...], gk_ref[...]
    qa0, ka0, va0 = 0, DIFF_QK_W, 2 * DIFF_QK_W
    qb0 = va0 + DIFF_V_W
    kb0 = qb0 + GQA_Q_W
    vb0 = kb0 + GQA_KV_W
    for j in range(DIFF_QK_W // HEAD_DIM):
        r = j * HEAD_DIM
        q = _rope_fm(pt_ref[qa0 + r:qa0 + r + HEAD_DIM, :], c1, s1) * Q_SCALE
        qt_ref[r:r + HEAD_DIM, :] = q.astype(BF16)
        kt_ref[r:r + HEAD_DIM, :] = _rope_fm(pt_ref[ka0 + r:ka0 + r + HEAD_DIM, :], c1, s1)
    for j in range(GQA_HEADS):
        r = j * HEAD_DIM
        q = _rope_fm(_rms_fm(pt_ref[qb0 + r:qb0 + r + HEAD_DIM, :], gq), c2, s2) * Q_SCALE
        qt_ref[DIFF_QK_W + r:DIFF_QK_W + r + HEAD_DIM, :] = q.astype(BF16)
    for j in range(GQA_KV_HEADS):
        r = j * HEAD_DIM
        kt_ref[DIFF_QK_W + r:DIFF_QK_W + r + HEAD_DIM, :] = _rope_fm(
            _rms_fm(pt_ref[kb0 + r:kb0 + r + HEAD_DIM, :], gk), c2, s2)
    for blk in range(EVEN_BLOCKS):
        rows = slice(blk * LANES, (blk + 1) * LANES)
        k_ref[blk] = kt_ref[rows, :].T.astype(BF16)
        v0 = va0 if blk < DIFF_HEADS else vb0 - DIFF_V_W
        vt_ref[blk] = pt_ref[v0 + blk * LANES:v0 + (blk + 1) * LANES, :].astype(BF16)


def _even_proj(x, sc, sh, g, w_in_t, rope, gq, gk, tm):
    b, s, d = x.shape
    c1, s1, c2, s2 = rope
    row = lambda bi, i: (bi, 0, 0)
    tab = pl.BlockSpec((HALF, tm), lambda bi, i: (0, i))
    full2 = lambda shp: pl.BlockSpec(shp, lambda bi, i: (0, 0))
    return pl.pallas_call(
        _even_proj_kernel,
        out_shape=(jax.ShapeDtypeStruct((b, EVEN_Q_W, s), BF16),
                   jax.ShapeDtypeStruct((b, EVEN_BLOCKS, s, LANES), BF16),
                   jax.ShapeDtypeStruct((b, s // tm, EVEN_BLOCKS, LANES, tm), BF16)),
        grid=(b, s // tm),
        in_specs=[
            pl.BlockSpec((None, tm, d), lambda bi, i: (bi, i, 0)),
            pl.BlockSpec((None, 1, d), row),
            pl.BlockSpec((None, 1, d), row),
            full2((1, d)),
            full2((EVEN_IN_W, d)),
            tab, tab, tab, tab,
            full2((HEAD_DIM, 1)),
            full2((HEAD_DIM, 1)),
        ],
        out_specs=(pl.BlockSpec((None, EVEN_Q_W, tm), lambda bi, i: (bi, 0, i)),
                   pl.BlockSpec((None, EVEN_BLOCKS, tm, LANES), lambda bi, i: (bi, 0, i, 0)),
                   pl.BlockSpec((None, None, EVEN_BLOCKS, LANES, tm), lambda bi, i: (bi, i, 0, 0, 0))),
        scratch_shapes=[pltpu.VMEM((EVEN_IN_W, tm), F32), pltpu.VMEM((EVEN_K_W, tm), F32)],
        compiler_params=_params(("parallel", "parallel")),
        name="even_proj",
    )(x, sc, sh, g, w_in_t, c1, s1, c2, s2, gq, gk)


def _even_attn_kernel(qt_ref, k_ref, vt_ref, lam_ref, sg_ref, o_ref,
                      rhs_ref, m_ref, l_ref, acc_ref, s0_ref, s1_ref, st0_ref, st1_ref,
                      *, tq, tk, vt_chunk, lambda_init):
    kv = pl.program_id(2)

    @pl.when(kv == 0)
    def _init():
        top = lax.broadcasted_iota(jnp.int32, (LANES, tq), 0) < HEAD_DIM
        zero = jnp.zeros((LANES, tq), BF16)
        for h in range(DIFF_HEADS):
            qh = qt_ref[h * LANES:(h + 1) * LANES, :]
            rhs_ref[2 * h] = jnp.where(top, qh, zero)
            rhs_ref[2 * h + 1] = jnp.where(top, zero, qh)
        for n in range(GQA_KV_HEADS):
            for g in range(GQA_GROUP):
                j = n * GQA_GROUP + g
                r = DIFF_QK_W + j * HEAD_DIM
                qg = jnp.concatenate([qt_ref[r:r + HEAD_DIM, :]] * 2, axis=0)
                keep = top if n == 0 else jnp.logical_not(top)
                rhs_ref[2 * DIFF_HEADS + j] = jnp.where(keep, qg, zero)
        m_ref[...] = jnp.full(m_ref.shape, NEG_BIG, F32)
        l_ref[...] = jnp.zeros(l_ref.shape, F32)
        acc_ref[...] = jnp.zeros(acc_ref.shape, F32)

    def scores(t, s_ref, st_ref):
        u = t & (N_UNITS - 1)
        key0 = (t >> UNIT_SHIFT) * tk
        blk = jnp.minimum(u >> 1, EVEN_BLOCKS - 1)
        rhs = rhs_ref[u]
        m8 = None
        for r in range(tk // KEY_CHUNK):
            krows = pl.ds(pl.multiple_of(key0 + r * KEY_CHUNK, KEY_CHUNK), KEY_CHUNK)
            st = jnp.dot(k_ref[blk, krows, :], rhs, preferred_element_type=F32)
            s_ref[r * KEY_CHUNK:(r + 1) * KEY_CHUNK, :] = st
            c8 = jnp.max(st.reshape(KEY_CHUNK // SUBLANES, SUBLANES, tq), axis=0)
            m8 = c8 if m8 is None else jnp.maximum(m8, c8)
        m_old = m_ref[u]
        m_new = jnp.maximum(m_old, jnp.max(m8, axis=0, keepdims=True))
        m_ref[u] = m_new
        st_ref[0] = m_new
        st_ref[1] = jnp.exp2(m_old - m_new)

    def update(t, s_ref, st_ref):
        u = t & (N_UNITS - 1)
        vchunk0 = (t >> UNIT_SHIFT) * (tk // vt_chunk)
        blk = jnp.minimum(u >> 1, EVEN_BLOCKS - 1)
        m_new = st_ref[0]
        alpha = st_ref[1]
        l8 = None
        pv = None
        for r in range(tk // KEY_CHUNK):
            p = jnp.exp2(s_ref[r * KEY_CHUNK:(r + 1) * KEY_CHUNK, :] - m_new)
            c8 = jnp.sum(p.reshape(KEY_CHUNK // SUBLANES, SUBLANES, tq), axis=0)
            l8 = c8 if l8 is None else l8 + c8
            off = (r * KEY_CHUNK) % vt_chunk
            vt = vt_ref[vchunk0 + (r * KEY_CHUNK) // vt_chunk, blk, :, off:off + KEY_CHUNK]
            d = jnp.dot(vt, p.astype(BF16), preferred_element_type=F32)
            pv = d if pv is None else pv + d
        l_ref[u] = alpha * l_ref[u] + jnp.sum(l8, axis=0, keepdims=True)
        acc_ref[u] = alpha * acc_ref[u] + pv

    n_items = (k_ref.shape[1] // tk) * N_UNITS
    scores(0, s0_ref, st0_ref)

    def pair(j, carry):
        t = 2 * j
        scores(t + 1, s1_ref, st1_ref)
        update(t, s0_ref, st0_ref)
        scores(t + 2, s0_ref, st0_ref)
        update(t + 1, s1_ref, st1_ref)
        return carry

    lax.fori_loop(0, n_items // 2 - 1, pair, 0)
    scores(n_items - 1, s1_ref, st1_ref)
    update(n_items - 2, s0_ref, st0_ref)
    update(n_items - 1, s1_ref, st1_ref)

    @pl.when(kv == pl.num_programs(2) - 1)
    def _finalize():
        lv = lam_ref[...]
        lam = (jnp.exp(jnp.sum(lv[0:1] * lv[1:2], axis=-1, keepdims=True))
               - jnp.exp(jnp.sum(lv[2:3] * lv[3:4], axis=-1, keepdims=True)) + lambda_init)
        for h in range(DIFF_HEADS):
            inv0 = 1.0 / l_ref[2 * h]
            inv1 = 1.0 / l_ref[2 * h + 1]
            ot = acc_ref[2 * h] * inv0 - lam * (acc_ref[2 * h + 1] * inv1)
            o = ot.T
            ms = jnp.mean(o * o, axis=-1, keepdims=True)
            o = o * lax.rsqrt(ms + SUBLN_EPS) * sg_ref[...] * (1.0 - lambda_init)
            o_ref[:, h * DIFF_VDIM:(h + 1) * DIFF_VDIM] = o.astype(o_ref.dtype)
        for j in range(GQA_HEADS // 2):
            u = 2 * DIFF_HEADS + 2 * j
            rows = slice((u - 2 * DIFF_HEADS) // GQA_GROUP * HEAD_DIM,
                         ((u - 2 * DIFF_HEADS) // GQA_GROUP + 1) * HEAD_DIM)
            o2 = jnp.concatenate([acc_ref[u, rows, :] * (1.0 / l_ref[u]),
                                  acc_ref[u + 1, rows, :] * (1.0 / l_ref[u + 1])], axis=0)
            c0 = DIFF_V_W + 2 * j * HEAD_DIM
            o_ref[:, c0:c0 + 2 * HEAD_DIM] = o2.T.astype(o_ref.dtype)


def _even_attn(qt, k, vt, lam_vec, subln_g, lambda_init, tq, tk, kvb):
    b, _, s = qt.shape
    vt_chunk = vt.shape[-1]
    assert kvb % tk == 0 and tk % vt_chunk == 0 and vt_chunk % KEY_CHUNK == 0
    kern = functools.partial(_even_attn_kernel, tq=tq, tk=tk, vt_chunk=vt_chunk,
                             lambda_init=lambda_init)
    return pl.pallas_call(
        kern,
        out_shape=jax.ShapeDtypeStruct((b, s, MIX_W), BF16),
        grid=(b, s // tq, s // kvb),
        in_specs=[
            pl.BlockSpec((None, EVEN_Q_W, tq), lambda bi, i, j: (bi, 0, i)),
            pl.BlockSpec((None, EVEN_BLOCKS, kvb, LANES), lambda bi, i, j: (bi, 0, j, 0)),
            pl.BlockSpec((None, kvb // vt_chunk, EVEN_BLOCKS, LANES, vt_chunk),
                         lambda bi, i, j: (bi, j, 0, 0, 0)),
            pl.BlockSpec((4, HEAD_DIM), lambda bi, i, j: (0, 0)),
            pl.BlockSpec((1, DIFF_VDIM), lambda bi, i, j: (0, 0)),
        ],
        out_specs=pl.BlockSpec((None, tq, MIX_W), lambda bi, i, j: (bi, i, 0)),
        scratch_shapes=[
            pltpu.VMEM((N_UNITS, LANES, tq), BF16),
            pltpu.VMEM((N_UNITS, 1, tq), F32),
            pltpu.VMEM((N_UNITS, 1, tq), F32),
            pltpu.VMEM((N_UNITS, LANES, tq), F32),
            pltpu.VMEM((tk, tq), F32),
            pltpu.VMEM((tk, tq), F32),
            pltpu.VMEM((2, 1, tq), F32),
            pltpu.VMEM((2, 1, tq), F32),
        ],
        compiler_params=_params(("parallel", "parallel", "arbitrary")),
        name="even_attn",
    )(qt, k, vt, lam_vec, subln_g)


def _out_proj_kernel(x_ref, o_ref, w_ref, gate_ref, y_ref):
    m = jnp.dot(o_ref[...], w_ref[...], preferred_element_type=F32)
    y_ref[...] = x_ref[...] + gate_ref[...] * m


def _out_proj(x, o, w, gate, tm):
    b, s, d = x.shape
    kdim = o.shape[-1]
    return pl.pallas_call(
        _out_proj_kernel,
        out_shape=jax.ShapeDtypeStruct((b, s, d), F32),
        grid=(b, s // tm),
        in_specs=[
            pl.BlockSpec((None, tm, d), lambda bi, i: (bi, i, 0)),
            pl.BlockSpec((None, tm, kdim), lambda bi, i: (bi, i, 0)),
            pl.BlockSpec((kdim, d), lambda bi, i: (0, 0)),
            pl.BlockSpec((None, 1, d), lambda bi, i: (bi, 0, 0)),
        ],
        out_specs=pl.BlockSpec((None, tm, d), lambda bi, i: (bi, i, 0)),
        compiler_params=_params(("parallel", "parallel")),
        name="out_proj",
    )(x, o, w, gate)


def _odd_proj_kernel(x_ref, sc_ref, sh_ref, g_ref, w_ref, q_ref, k_ref, v_ref):
    h = _norm_mod(x_ref[...], g_ref[...], 1.0 + sc_ref[...], sh_ref[...]).astype(BF16)
    d = q_ref.shape[-1]
    q_ref[...] = (jnp.dot(h, w_ref[:, 0:d], preferred_element_type=F32) * Q_SCALE).astype(BF16)
    k_ref[...] = jnp.dot(h, w_ref[:, d:2 * d], preferred_element_type=F32).astype(BF16)
    v_ref[...] = jnp.dot(h, w_ref[:, 2 * d:3 * d], preferred_element_type=F32).astype(BF16)


def _odd_proj(x, sc, sh, g, w, tm):
    b, s, d = x.shape
    row = lambda bi, i: (bi, 0, 0)
    tok = pl.BlockSpec((None, tm, d), lambda bi, i: (bi, i, 0))
    shp = jax.ShapeDtypeStruct((b, s, d), BF16)
    return pl.pallas_call(
        _odd_proj_kernel,
        out_shape=(shp, shp, shp),
        grid=(b, s // tm),
        in_specs=[
            tok,
            pl.BlockSpec((None, 1, d), row),
            pl.BlockSpec((None, 1, d), row),
            pl.BlockSpec((1, d), lambda bi, i: (0, 0)),
            pl.BlockSpec((d, 3 * d), lambda bi, i: (0, 0)),
        ],
        out_specs=(tok, tok, tok),
        compiler_params=_params(("parallel", "parallel")),
        name="odd_proj",
    )(x, sc, sh, g, w)


NA_ROWS_PER_STEP = 8
NA_ROWS_PER_ITER = 2
NA_LOOKAHEAD = 2
NA_BLK = NA_ROWS_PER_STEP * GRID_W
NA_KEYS = NA_KH * GRID_W


def _na_kernel(q_ref, kp_ref, kc_ref, kn_ref, vp_ref, vc_ref, vn_ref, tb_ref, o_ref,
               kbuf, vbuf, sbuf, *, rows):
    i = pl.program_id(1)
    kbuf[0:NA_BLK, :] = kp_ref[...]
    kbuf[NA_BLK:2 * NA_BLK, :] = kc_ref[...]
    kbuf[2 * NA_BLK:3 * NA_BLK, :] = kn_ref[...]
    vbuf[0:NA_BLK, :] = vp_ref[...]
    vbuf[NA_BLK:2 * NA_BLK, :] = vc_ref[...]
    vbuf[2 * NA_BLK:3 * NA_BLK, :] = vn_ref[...]
    r0 = i * NA_ROWS_PER_STEP
    left = lax.broadcasted_iota(jnp.int32, (GRID_W, 2 * HEAD_DIM), 1) < HEAD_DIM

    def row_params(j):
        r = r0 + j
        rs = jnp.clip(r - NA_KH // 2, 0, rows - NA_KH)
        start = pl.multiple_of((rs - r0 + NA_ROWS_PER_STEP) * GRID_W, GRID_W)
        d0 = rs - r + (NA_KH - 1)
        qoff = pl.multiple_of(j * GRID_W, GRID_W)
        return start, d0, qoff

    def qk(rp, pr):
        start, _, qoff = rp
        cols = slice(pr * 2 * HEAD_DIM, (pr + 1) * 2 * HEAD_DIM)
        q2 = q_ref[pl.ds(qoff, GRID_W), cols]
        zero = jnp.zeros_like(q2)
        lhs = jnp.concatenate([jnp.where(left, q2, zero), jnp.where(left, zero, q2)], axis=0)
        k2 = kbuf[pl.ds(start, NA_KEYS), cols]
        return lax.dot_general(lhs, k2, NT_DIMS, preferred_element_type=F32)

    def finish(rp, pr, sc):
        start, d0, qoff = rp
        cols = slice(pr * 2 * HEAD_DIM, (pr + 1) * 2 * HEAD_DIM)
        v2 = vbuf[pl.ds(start, NA_KEYS), cols]
        bias = jnp.concatenate(
            [jnp.concatenate([tb_ref[2 * pr + e, d0 + 2 * t] for t in range(NA_KH // 2)], axis=1)
             for e in range(2)], axis=0)
        sc = sc + bias
        m = jnp.max(sc, axis=-1, keepdims=True)
        p = jnp.exp2(sc - m)
        l = jnp.sum(p, axis=-1, keepdims=True)
        res = jnp.dot(p.astype(BF16), v2, preferred_element_type=F32) * (1.0 / l)
        o2 = jnp.where(left, res[0:GRID_W], res[GRID_W:2 * GRID_W])
        o_ref[pl.ds(qoff, GRID_W), cols] = o2.astype(o_ref.dtype)

    def row_group(jj, carry):
        items = [(row_params(jj * NA_ROWS_PER_ITER + e), pr)
                 for e in range(NA_ROWS_PER_ITER) for pr in range(NA_HEADS // 2)]
        nslot = NA_LOOKAHEAD + 1
        for n in range(NA_LOOKAHEAD):
            sbuf[n % nslot] = qk(*items[n])
        for n, item in enumerate(items):
            if n + NA_LOOKAHEAD < len(items):
                sbuf[(n + NA_LOOKAHEAD) % nslot] = qk(*items[n + NA_LOOKAHEAD])
            finish(*item, sbuf[n % nslot])
        return carry

    lax.fori_loop(0, NA_ROWS_PER_STEP // NA_ROWS_PER_ITER, row_group, 0)


def _na_attn(q, k, v, tb2):
    b, s, d = q.shape
    rows = s // GRID_W
    nblk = rows // NA_ROWS_PER_STEP
    kern = functools.partial(_na_kernel, rows=rows)
    cur = pl.BlockSpec((None, NA_BLK, d), lambda bi, i: (bi, i, 0))
    prev = pl.BlockSpec((None, NA_BLK, d), lambda bi, i: (bi, jnp.maximum(i - 1, 0), 0))
    nxt = pl.BlockSpec((None, NA_BLK, d), lambda bi, i: (bi, jnp.minimum(i + 1, nblk - 1), 0))
    return pl.pallas_call(
        kern,
        out_shape=jax.ShapeDtypeStruct((b, s, d), BF16),
        grid=(b, nblk),
        in_specs=[cur, prev, cur, nxt, prev, cur, nxt,
                  pl.BlockSpec(tb2.shape, lambda bi, i: (0, 0, 0, 0))],
        out_specs=cur,
        scratch_shapes=[pltpu.VMEM((3 * NA_BLK, d), BF16), pltpu.VMEM((3 * NA_BLK, d), BF16),
                        pltpu.VMEM((NA_LOOKAHEAD + 1, 2 * GRID_W, NA_KEYS), F32)],
        compiler_params=_params(("parallel", "parallel")),
        name="na_attn",
    )(q, k, k, k, v, v, v, tb2)


def _na_bias_table(rpb):
    qc = jnp.arange(GRID_W, dtype=jnp.int32)[:, None]
    kc = jnp.arange(GRID_W, dtype=jnp.int32)[None, :]
    cs = jnp.clip(qc - NA_KW // 2, 0, GRID_W - NA_KW)
    valid = (kc >= cs) & (kc < cs + NA_KW)
    dc = jnp.clip(kc - qc + (NA_KW - 1), 0, 2 * NA_KW - 2)
    t = jnp.where(valid[None, None], rpb[:, :, dc].astype(F32) * LOG2E, NEG_BIG)
    return jnp.concatenate([t[:, :-1], t[:, 1:]], axis=-1)


FFN_HALO = SUBLANES


def _ffn_kernel(xp_ref, x_ref, xn_ref, sc_ref, sh_ref, gate_ref, g_ref, wg_ref, wv_ref,
                cwg_ref, cwv_ref, cbg_ref, cbv_ref, wd_ref, fg_ref, y_ref,
                h_ref, ua_ref, ub_ref, acc_ref, *, tm, final_norm):
    i = pl.program_id(1)
    last = pl.num_programs(1) - 1
    xe = jnp.concatenate([xp_ref[...], x_ref[...], xn_ref[...]], axis=0)
    h = _norm_mod(xe, g_ref[...], 1.0 + sc_ref[...], sh_ref[...])
    row = lax.broadcasted_iota(jnp.int32, (tm + 2 * FFN_HALO, 1), 0)
    inside = ((row >= FFN_HALO) | (i > 0)) & ((row < tm + FFN_HALO) | (i < last))
    h_ref[...] = jnp.where(inside, h, 0.0).astype(BF16)
    acc_ref[...] = jnp.zeros(acc_ref.shape, F32)

    def conv(u_ref, cw, cb):
        return (u_ref[FFN_HALO - 1:FFN_HALO - 1 + tm, :] * cw[0:1]
                + u_ref[FFN_HALO:FFN_HALO + tm, :] * cw[1:2]
                + u_ref[FFN_HALO + 1:FFN_HALO + 1 + tm, :] * cw[2:3] + cb)

    def up(c, u_ref):
        hb = h_ref[...]
        u_ref[0] = jnp.dot(hb, wg_ref[c], preferred_element_type=F32)
        u_ref[1] = jnp.dot(hb, wv_ref[c], preferred_element_type=F32)

    def down(c, u_ref):
        gt = conv(u_ref.at[0], cwg_ref[c], cbg_ref[c])
        vl = conv(u_ref.at[1], cwv_ref[c], cbv_ref[c])
        act = (gt * (1.0 / (1.0 + jnp.exp(-gt))) * vl).astype(BF16)
        acc_ref[...] += jnp.dot(act, wd_ref[c], preferred_element_type=F32)

    nc = wg_ref.shape[0]
    assert nc % 2 == 1
    up(0, ua_ref)

    def pair(j, carry):
        c = 2 * j
        up(c + 1, ub_ref)
        down(c, ua_ref)
        up(c + 2, ua_ref)
        down(c + 1, ub_ref)
        return carry

    lax.fori_loop(0, nc // 2, pair, 0)
    down(nc - 1, ua_ref)
    y = x_ref[...] + gate_ref[...] * acc_ref[...]
    if final_norm:
        ms = jnp.mean(y * y, axis=-1, keepdims=True)
        y = y * lax.rsqrt(ms + EPS) * fg_ref[...]
    y_ref[...] = y


def _ffn(x, sc, sh, gate, g, ffn_w, final_g, final_norm, tm):
    wg, wv, cwg, cwv, cbg, cbv, wd = ffn_w
    b, s, d = x.shape
    nc, _, tn = wg.shape
    hb = tm // FFN_HALO
    nh = s // FFN_HALO
    kern = functools.partial(_ffn_kernel, tm=tm, final_norm=final_norm)
    row = lambda bi, i: (bi, 0, 0)
    c3 = lambda bi, i: (0, 0, 0)
    return pl.pallas_call(
        kern,
        out_shape=jax.ShapeDtypeStruct((b, s, d), F32),
        grid=(b, s // tm),
        in_specs=[
            pl.BlockSpec((None, FFN_HALO, d), lambda bi, i: (bi, jnp.maximum(i * hb - 1, 0), 0)),
            pl.BlockSpec((None, tm, d), lambda bi, i: (bi, i, 0)),
            pl.BlockSpec((None, FFN_HALO, d), lambda bi, i: (bi, jnp.minimum((i + 1) * hb, nh - 1), 0)),
            pl.BlockSpec((None, 1, d), row),
            pl.BlockSpec((None, 1, d), row),
            pl.BlockSpec((None, 1, d), row),
            pl.BlockSpec((1, d), lambda bi, i: (0, 0)),
            pl.BlockSpec(wg.shape, c3),
            pl.BlockSpec(wv.shape, c3),
            pl.BlockSpec(cwg.shape, c3),
            pl.BlockSpec(cwv.shape, c3),
            pl.BlockSpec(cbg.shape, c3),
            pl.BlockSpec(cbv.shape, c3),
            pl.BlockSpec(wd.shape, c3),
            pl.BlockSpec((1, d), lambda bi, i: (0, 0)),
        ],
        out_specs=pl.BlockSpec((None, tm, d), lambda bi, i: (bi, i, 0)),
        scratch_shapes=[
            pltpu.VMEM((tm + 2 * FFN_HALO, d), BF16),
            pltpu.VMEM((2, tm + 2 * FFN_HALO, tn), F32),
            pltpu.VMEM((2, tm + 2 * FFN_HALO, tn), F32),
            pltpu.VMEM((tm, d), F32),
        ],
        compiler_params=_params(("parallel", "parallel")),
        name="conv_ffn",
    )(x, x, x, sc, sh, gate, g, wg, wv, cwg, cwv, cbg, cbv, wd, final_g)


def _ffn_weights(w_up, conv_w, conv_b, w_down, tn):
    d = w_up.shape[0]
    nc = D_FF // tn

    def cols(a):
        return a.reshape(a.shape[0], nc, tn).transpose(1, 0, 2)

    wg = cols(w_up[:, :D_FF]).astype(BF16)
    wv = cols(w_up[:, D_FF:]).astype(BF16)
    cwg = cols(conv_w[:, :D_FF])
    cwv = cols(conv_w[:, D_FF:])
    cbg = cols(conv_b[None, :D_FF])
    cbv = cols(conv_b[None, D_FF:])
    wd = w_down.reshape(nc, tn, d).astype(BF16)
    return wg, wv, cwg, cwv, cbg, cbv, wd


def _rope_tables(s):
    inv1 = ROPE_THETA ** (-jnp.arange(0, HEAD_DIM, 2, dtype=F32) / HEAD_DIM)
    t = jnp.arange(s, dtype=F32)
    ang1 = t[:, None] * inv1[None, :]
    inv2 = ROPE_THETA ** (-jnp.arange(0, HALF, 2, dtype=F32) / HALF)
    ti = jnp.arange(s, dtype=jnp.int32)
    rowf = (ti // GRID_W).astype(F32)
    colf = (ti % GRID_W).astype(F32)
    ang2 = jnp.concatenate([rowf[:, None] * inv2[None, :], colf[:, None] * inv2[None, :]], axis=-1)
    return jnp.cos(ang1).T, jnp.sin(ang1).T, jnp.cos(ang2).T, jnp.sin(ang2).T


def _tile(s, pref):
    t = min(pref, s)
    assert s % t == 0
    return t


def _trunk(x, c, p):
    b, s, d = x.shape
    depth = p["ada_w"].shape[0]
    tm = _tile(s, 512)
    tq = _tile(s, 512)
    tk = _tile(s, 1024)
    kvb = _tile(s, 4096)
    mod = _ada_mod(c, p["ada_w"], p["ada_b"])
    rope = _rope_tables(s)
    for l in range(depth):
        sh1, sc1, g1, sh2, sc2, g2 = [mod[l, :, None, k * d:(k + 1) * d] for k in range(6)]
        ng = p["norm_g"][l]
        if l % 2 == 0:
            e = l // 2
            lambda_init = 0.8 - 0.6 * math.exp(-0.3 * l)
            qt, k, vt = _even_proj(x, sc1, sh1, ng[0:1], p["even_w_in_t"][e], rope,
                                   p["gqa_qk_g"][e, 0][:, None], p["gqa_qk_g"][e, 1][:, None], tm)
            o = _even_attn(qt, k, vt, p["diff_lambda"][e], p["diff_subln_g"][e][None, :],
                           lambda_init, tq, tk, kvb)
            w_out = p["even_w_out"][e]
        else:
            j = l // 2
            q, k, v = _odd_proj(x, sc1, sh1, ng[0:1], p["odd_w_qkv"][j], tm)
            o = _na_attn(q, k, v, p["odd_tb2"][j])
            w_out = p["odd_w_out"][j]
        x = _out_proj(x, o, w_out, g1, tm)
        x = _ffn(x, sc2, sh2, g2, ng[1:2], p["ffn"][l], p["final_g"], l == depth - 1, tm)
    return x


def kernel(x_prompt, x_sample, c_prompt, c_sample, ada_w, ada_b, norm_g, even_w_in, even_w_out, diff_lambda, diff_subln_g, gqa_qk_g, odd_w_qkv, odd_rpb, odd_w_out, ffn_w_up, ffn_conv_w, ffn_conv_b, ffn_w_down, final_g):
    depth = ada_w.shape[0]
    p = {
        "ada_w": ada_w, "ada_b": ada_b, "norm_g": norm_g,
        "even_w_in_t": jnp.swapaxes(even_w_in, 1, 2).astype(BF16),
        "even_w_out": even_w_out.astype(BF16),
        "diff_lambda": diff_lambda, "diff_subln_g": diff_subln_g, "gqa_qk_g": gqa_qk_g,
        "odd_w_qkv": odd_w_qkv.astype(BF16),
        "odd_tb2": [_na_bias_table(odd_rpb[j]) for j in range(odd_rpb.shape[0])],
        "odd_w_out": odd_w_out.astype(BF16),
        "ffn": [_ffn_weights(ffn_w_up[l], ffn_conv_w[l], ffn_conv_b[l], ffn_w_down[l], 256)
                for l in range(depth)],
        "final_g": final_g[None, :],
    }
    return (_trunk(x_prompt, c_prompt, p), _trunk(x_sample, c_sample, p))
```
